```python
import jax, jax.numpy as jnp
from jax import lax
import numpy as np

D_MODEL = 1024
BATCH = 8
SEQ = 2048
DEPTH = 1

MIX_WIDTH = D_MODEL
HG_WIDTH = MIX_WIDTH // 2
HG_HEADS = 4
HG_DIM = HG_WIDTH // HG_HEADS
ATT_WIDTH = MIX_WIDTH - HG_WIDTH
ATT_HEADS = 8
ATT_DIM = ATT_WIDTH // ATT_HEADS
BLOCK = 256
TOPK = 3
Q_CHUNK = 32
HG_CHUNK = 64
D_FF = -(-8 * D_MODEL // (3 * 256)) * 256
IN_COLS = 4 * HG_WIDTH + 3 * ATT_WIDTH
EPS = 1e-6

kernel_name = "hymba_hgrn2_moba_alibi_adaln_block"


def rms_norm(t, gain):
    tf = t.astype(jnp.float32)
    tf = tf * lax.rsqrt(jnp.mean(tf * tf, axis=-1, keepdims=True) + EPS)
    return (tf * gain.astype(jnp.float32)).astype(t.dtype)


def split_heads(t, n):
    B, T, _ = t.shape
    return t.reshape(B, T, n, -1).transpose(0, 2, 1, 3)


def merge_heads(t):
    B, H, T, d = t.shape
    return t.transpose(0, 2, 1, 3).reshape(B, T, H * d)


def hgrn2_mixer(q, f_logit, i, g, lb, out_gain):
    B, H, T, d = q.shape
    f32 = jnp.float32
    lbb = lb[None, :, None, :]
    f = lbb + (1.0 - lbb) * jax.nn.sigmoid(f_logit.astype(f32))
    log_f = jnp.log(f)
    k = 1.0 - f
    qf = jax.nn.silu(q.astype(f32))
    v = i.astype(f32)
    nc = T // HG_CHUNK

    def chunks(t):
        return t.reshape(B, H, nc, HG_CHUNK, d).transpose(2, 0, 1, 3, 4)

    tri = jnp.tril(jnp.ones((HG_CHUNK, HG_CHUNK), dtype=bool))[None, None, :, :, None]

    def step(S, inp):
        qc, kc, vc, lfc = inp
        b = jnp.cumsum(lfc, axis=2)
        diff = b[:, :, :, None, :] - b[:, :, None, :, :]
        decay = jnp.where(tri, jnp.exp(jnp.where(tri, diff, 0.0)), 0.0)
        attn = jnp.einsum('bhtsk,bhsk->bhts', decay * qc[:, :, :, None, :], kc)
        o = jnp.einsum('bhts,bhsv->bhtv', attn, vc) + jnp.einsum('bhtk,bhkv->bhtv', qc * jnp.exp(b), S)
        b_last = b[:, :, -1:, :]
        S = jnp.exp(b_last[:, :, 0, :, None]) * S + jnp.einsum('bhsk,bhsv->bhkv', kc * jnp.exp(b_last - b), vc)
        return S, o

    S0 = jnp.zeros((B, H, d, d), f32)
    _, o = lax.scan(step, S0, (chunks(qf), chunks(k), chunks(v), chunks(log_f)))
    o = o.transpose(1, 2, 0, 3, 4).reshape(B, H, T, d)
    o = rms_norm(o, out_gain) * jax.nn.silu(g.astype(f32))
    return o.astype(q.dtype)


def moba_mixer(q, k, v, q_gain, k_gain):
    B, H, T, dh = q.shape
    f32 = jnp.float32
    q = rms_norm(q, q_gain)
    k = rms_norm(k, k_gain)
    nb = -(-T // BLOCK)
    pad = nb * BLOCK - T
    kp = jnp.pad(k, ((0, 0), (0, 0), (0, pad), (0, 0)))
    vp = jnp.pad(v, ((0, 0), (0, 0), (0, pad), (0, 0)))
    kb = kp.reshape(B, H, nb, BLOCK, dh)
    vb = vp.reshape(B, H, nb, BLOCK, dh)

    k_mean = jnp.mean(kb.astype(f32), axis=3)
    gate = jnp.einsum('bhtd,bhnd->bhtn', q.astype(f32), k_mean)
    pos = jnp.arange(T)
    qblk = pos // BLOCK
    past = jnp.arange(nb)[None, :] < qblk[:, None]
    gate = jnp.where(past[None, None], gate, -jnp.inf)
    n_sel = min(TOPK, nb)
    _, idx = lax.top_k(gate, n_sel)
    sel_valid = jnp.arange(n_sel)[None, :] < qblk[:, None]

    slopes = jnp.exp2(-8.0 * jnp.arange(1, H + 1, dtype=f32) / H)
    scale = dh ** -0.5
    bi = jnp.arange(B)[:, None, None, None]
    hi = jnp.arange(H)[None, :, None, None]
    offs = jnp.arange(BLOCK)

    def chunk_fn(cidx):
        t0 = cidx * Q_CHUNK
        qc = lax.dynamic_slice_in_dim(q, t0, Q_CHUNK, axis=2)
        ic = lax.dynamic_slice_in_dim(idx, t0, Q_CHUNK, axis=2)
        vmask = lax.dynamic_slice_in_dim(sel_valid, t0, Q_CHUNK, axis=0)
        tq = (t0 + jnp.arange(Q_CHUNK)).astype(f32)
        kg = kb[bi, hi, ic]
        vg = vb[bi, hi, ic]
        sp = (ic[..., None] * BLOCK + offs).astype(f32)
        s_past = jnp.einsum('bhqd,bhqnkd->bhqnk', qc, kg).astype(f32) * scale
        s_past = s_past - slopes[None, :, None, None, None] * (tq[None, None, :, None, None] - sp)
        s_past = jnp.where(vmask[None, None, :, :, None], s_past, -jnp.inf)
        j0 = (t0 // BLOCK) * BLOCK
        ko = lax.dynamic_slice_in_dim(kp, j0, BLOCK, axis=2)
        vo = lax.dynamic_slice_in_dim(vp, j0, BLOCK, axis=2)
        so = (j0 + offs).astype(f32)
        s_own = jnp.einsum('bhqd,bhkd->bhqk', qc, ko).astype(f32) * scale
        s_own = s_own - slopes[None, :, None, None] * (tq[:, None] - so[None, :])[None, None]
        s_own = jnp.where((so[None, :] <= tq[:, None])[None, None], s_own, -jnp.inf)
        logits = jnp.concatenate([s_past.reshape(B, H, Q_CHUNK, n_sel * BLOCK), s_own], axis=-1)
        p = jax.nn.softmax(logits, axis=-1).astype(v.dtype)
        p_past = p[..., :n_sel * BLOCK].reshape(B, H, Q_CHUNK, n_sel, BLOCK)
        p_own = p[..., n_sel * BLOCK:]
        return (jnp.einsum('bhqnk,bhqnkd->bhqd', p_past, vg)
                + jnp.einsum('bhqk,bhkd->bhqd', p_own, vo))

    out = lax.map(chunk_fn, jnp.arange(T // Q_CHUNK))
    return out.transpose(1, 2, 0, 3, 4).reshape(B, H, T, dh)


def setup_inputs(seed: int = 0) -> dict:
    key = jax.random.key(seed)
    ks = jax.random.split(key, 20)
    f32 = jnp.float32

    def w(k, shape, fan_in, mult=1.0):
        return jax.random.normal(k, shape, f32) * (mult * fan_in ** -0.5)

    def gain(k, shape):
        return 1.0 + 0.05 * jax.random.normal(k, shape, f32)

    return {
        "x": jax.random.normal(ks[0], (BATCH, SEQ, D_MODEL), f32),
        "c": jax.random.normal(ks[1], (BATCH, D_MODEL), f32),
        "w_ada": w(ks[2], (DEPTH, D_MODEL, 6 * D_MODEL), D_MODEL, 0.5),
        "b_ada": 0.02 * jax.random.normal(ks[3], (DEPTH, 6 * D_MODEL), f32),
        "norm1_g": gain(ks[4], (DEPTH, D_MODEL)),
        "w_in": w(ks[5], (DEPTH, D_MODEL, IN_COLS), D_MODEL),
        "lb_logits": 0.5 * jax.random.normal(ks[6], (DEPTH + 1, HG_WIDTH), f32),
        "hg_norm_g": gain(ks[7], (DEPTH, HG_DIM)),
        "q_norm_g": gain(ks[8], (DEPTH, ATT_DIM)),
        "k_norm_g": gain(ks[9], (DEPTH, ATT_DIM)),
        "w_out": w(ks[10], (DEPTH, MIX_WIDTH, D_MODEL), MIX_WIDTH),
        "norm2_g": gain(ks[11], (DEPTH, D_MODEL)),
        "w_gate": w(ks[12], (DEPTH, D_MODEL, D_FF), D_MODEL),
        "w_up": w(ks[13], (DEPTH, D_MODEL, D_FF), D_MODEL),
        "w_down": w(ks[14], (DEPTH, D_FF, D_MODEL), D_FF),
    }


def reference(x, c, w_ada, b_ada, norm1_g, w_in, lb_logits, hg_norm_g, q_norm_g, k_norm_g,
              w_out, norm2_g, w_gate, w_up, w_down):
    lbs = jnp.cumsum(jax.nn.softmax(lb_logits.astype(jnp.float32), axis=0), axis=0)
    c_act = jax.nn.silu(c)
    for l in range(DEPTH):
        mod = c_act @ w_ada[l] + b_ada[l]
        shift1, scale1, gate1, shift2, scale2, gate2 = [m[:, None, :] for m in jnp.split(mod, 6, axis=-1)]

        h = rms_norm(x, norm1_g[l]) * (1.0 + scale1) + shift1
        proj = h @ w_in[l]
        hq, hf, hi_, hg, aq, ak, av = jnp.split(
            proj, np.cumsum([HG_WIDTH] * 4 + [ATT_WIDTH] * 2).tolist(), axis=-1)
        lb = lbs[l].reshape(HG_HEADS, HG_DIM)
        o_hg = hgrn2_mixer(split_heads(hq, HG_HEADS), split_heads(hf, HG_HEADS),
                           split_heads(hi_, HG_HEADS), split_heads(hg, HG_HEADS),
                           lb, hg_norm_g[l])
        o_att = moba_mixer(split_heads(aq, ATT_HEADS), split_heads(ak, ATT_HEADS),
                           split_heads(av, ATT_HEADS), q_norm_g[l], k_norm_g[l])
        mixed = jnp.concatenate([merge_heads(o_hg), merge_heads(o_att)], axis=-1)
        x = x + gate1 * (mixed @ w_out[l])

        h2 = rms_norm(x, norm2_g[l]) * (1.0 + scale2) + shift2
        ffn = (jax.nn.silu(h2 @ w_gate[l]) * (h2 @ w_up[l])) @ w_down[l]
        x = x + gate2 * ffn
    return x
```

```python
import functools

import jax
import jax.numpy as jnp
from jax import lax
from jax.experimental import pallas as pl
from jax.experimental.pallas import tpu as pltpu

F32 = jnp.float32
BF16 = jnp.bfloat16

HG_HEADS = 4
ATT_HEADS = 8
BLOCK = 256
TOPK = 3
HG_CHUNK = 64
EPS = 1e-6
NEG = -1e30

VMEM_LIMIT = 56 * 1024 * 1024


def _silu(t):
    return t * jax.nn.sigmoid(t)


def _dot(a, b):
    return jnp.dot(a, b, preferred_element_type=F32)


def _dot_nt(a, b):
    return lax.dot_general(a, b, (((1,), (1,)), ((), ())), preferred_element_type=F32)


def _dot_tn(a, b):
    return lax.dot_general(a, b, (((0,), (0,)), ((), ())), preferred_element_type=F32)


def _split2(t):
    hi = t.astype(BF16)
    lo = (t - hi.astype(F32)).astype(BF16)
    return hi, lo


def _dot3(a, b, dot=_dot):
    a_hi, a_lo = _split2(a)
    b_hi, b_lo = _split2(b)
    return (dot(a_hi, b_lo) + dot(a_lo, b_hi)) + dot(a_hi, b_hi)


def _adaln_kernel(c_ref, w_ref, b_ref, o_ref):
    o_ref[...] = _dot3(_silu(c_ref[...]), w_ref[...]) + b_ref[...]


def _adaln(c, w, b):
    bsz, d = c.shape
    n = w.shape[1]
    tn = 1024
    return pl.pallas_call(
        _adaln_kernel,
        out_shape=jax.ShapeDtypeStruct((bsz, n), F32),
        grid=(n // tn,),
        in_specs=[pl.BlockSpec((bsz, d), lambda j: (0, 0)),
                  pl.BlockSpec((d, tn), lambda j: (0, j)),
                  pl.BlockSpec((1, tn), lambda j: (0, j))],
        out_specs=pl.BlockSpec((bsz, tn), lambda j: (0, j)),
        name="adaln",
    )(c, w, b.reshape(1, n))


def _inproj_kernel(x_ref, mod_ref, g1_ref, w_ref, lbl_ref, qg_ref, kg_ref, pool_ref,
                   hq_ref, kk_ref, lf_ref, hv_ref, hg_ref, aq_ref, ak_ref, av_ref, gate_ref,
                   kmean_scr, *, tm, hw, aw, layer):
    i = pl.program_id(1)
    dh = aw // ATT_HEADS
    nb = kmean_scr.shape[0]

    x = x_ref[0]
    xn = x * lax.rsqrt(jnp.mean(x * x, axis=-1, keepdims=True) + EPS)
    mod = mod_ref[0]
    h = (xn * g1_ref[...]) * (1.0 + mod[1:2]) + mod[0:1]
    hb = h.astype(BF16)

    def proj(lo, width):
        return _dot(hb, w_ref[:, lo:lo + width])

    hq_ref[0] = _silu(proj(0, hw)).astype(BF16)
    lbl = lbl_ref[...]
    e = jnp.exp(lbl - jnp.max(lbl, axis=0, keepdims=True))
    lb = jnp.sum(e[0:layer + 1], axis=0, keepdims=True) / jnp.sum(e, axis=0, keepdims=True)
    f = lb + (1.0 - lb) * jax.nn.sigmoid(proj(hw, hw))
    lf_ref[0] = jnp.log(f)
    kk_ref[0] = (1.0 - f).astype(BF16)
    hv_ref[0] = proj(2 * hw, hw).astype(BF16)
    hg_ref[0] = _silu(proj(3 * hw, hw)).astype(BF16)

    def head_norm(t, gain):
        ms = _dot((t * t).astype(BF16), pool_ref[...])
        return t * lax.rsqrt(ms + EPS) * gain

    qn = head_norm(proj(4 * hw, aw), qg_ref[...])
    kn = head_norm(proj(4 * hw + aw, aw), kg_ref[...])
    aq_ref[0] = (qn * dh ** -0.5).astype(BF16)
    ak_ref[0] = kn.astype(BF16)
    av_ref[0] = proj(4 * hw + 2 * aw, aw).astype(BF16)

    @pl.when(i == 0)
    def _():
        kmean_scr[...] = jnp.zeros_like(kmean_scr)

    per_tile = tm // BLOCK
    for s in range(per_tile):
        kmean_scr[pl.ds(i * per_tile + s, 1), :] = jnp.mean(
            kn[s * BLOCK:(s + 1) * BLOCK], axis=0, keepdims=True)

    kmean = kmean_scr[...]
    rows = lax.broadcasted_iota(jnp.int32, (ATT_HEADS * nb, aw), 0)
    cols = lax.broadcasted_iota(jnp.int32, (ATT_HEADS * nb, aw), 1)
    sel = jnp.where(rows // nb == cols // dh, jnp.concatenate([kmean] * ATT_HEADS, axis=0), 0.0)
    gate = _dot3(qn, sel, dot=_dot_nt)
    pw = 2 * nb
    for p in range(ATT_HEADS // 2):
        gate_ref[0, p] = gate[:, p * pw:(p + 1) * pw]


def _inproj(x, mod, g1, w_in, lb_logits, qg, kg, *, layer, tm=512):
    bsz, t, d = x.shape
    hw = lb_logits.shape[1]
    aw = (w_in.shape[1] - 4 * hw) // 3
    dh = aw // ATT_HEADS
    nb = t // BLOCK
    pool = jnp.where(jnp.arange(aw)[:, None] // dh == jnp.arange(aw)[None, :] // dh, 1.0 / dh, 0.0).astype(BF16)
    tok = lambda w: pl.BlockSpec((1, tm, w), lambda b, i: (b, i, 0))
    const = lambda shape: pl.BlockSpec(shape, lambda b, i: (0,) * len(shape), pipeline_mode=pl.Buffered(1))
    bf = lambda w: jax.ShapeDtypeStruct((bsz, t, w), BF16)
    return pl.pallas_call(
        functools.partial(_inproj_kernel, tm=tm, hw=hw, aw=aw, layer=layer),
        out_shape=[bf(hw), bf(hw), jax.ShapeDtypeStruct((bsz, t, hw), F32), bf(hw), bf(hw),
                   bf(aw), bf(aw), bf(aw),
                   jax.ShapeDtypeStruct((bsz, ATT_HEADS // 2, t, 2 * nb), F32)],
        grid=(bsz, t // tm),
        in_specs=[tok(d),
                  pl.BlockSpec((1, 6, d), lambda b, i: (b, 0, 0)),
                  const((1, d)), const(w_in.shape), const(lb_logits.shape),
                  const((1, aw)), const((1, aw)), const((aw, aw))],
        out_specs=[tok(hw), tok(hw), tok(hw), tok(hw), tok(hw), tok(aw), tok(aw), tok(aw),
                   pl.BlockSpec((1, ATT_HEADS // 2, tm, 2 * nb), lambda b, i: (b, 0, i, 0))],
        scratch_shapes=[pltpu.VMEM((nb, aw), F32)],
        compiler_params=pltpu.CompilerParams(
            dimension_semantics=("parallel", "arbitrary"), vmem_limit_bytes=VMEM_LIMIT),
        name="inproj",
    )(x, mod, g1.reshape(1, d), w_in.astype(BF16), lb_logits,
      jnp.tile(qg, ATT_HEADS).reshape(1, aw), jnp.tile(kg, ATT_HEADS).reshape(1, aw), pool)


HG_LEVELS = (32, 16, 8)
HG_DIAG = 8


def _group_rows(b, first, step, rows):
    n = b.shape[0] // rows
    return jnp.concatenate(
        [jnp.broadcast_to(b[first + g * step:first + g * step + 1], (rows, b.shape[1])) for g in range(n)], axis=0)


def _hgrn_kernel(q_ref, k_ref, lf_ref, v_ref, g_ref, gain_ref, o_ref, st_ref):
    c_len = HG_CHUNK
    t_len = q_ref.shape[1]
    r = lax.broadcasted_iota(jnp.int32, (c_len, c_len), 0)
    c = lax.broadcasted_iota(jnp.int32, (c_len, c_len), 1)
    tri = (r >= c).astype(BF16)
    level_masks = [(r // (2 * m) == c // (2 * m)) & ((r // m) % 2 == 1) & ((c // m) % 2 == 0) for m in HG_LEVELS]
    diag_mask = (r // HG_DIAG == c // HG_DIAG) & (r >= c)
    gain = gain_ref[...]

    st_ref[...] = jnp.zeros_like(st_ref)

    def body(ci, carry):
        sl = pl.ds(pl.multiple_of(ci * c_len, c_len), c_len)
        q = q_ref[0, sl, :].astype(F32)
        kk = k_ref[0, sl, :].astype(F32)
        lf = lf_ref[0, sl, :]
        v = v_ref[0, sl, :]

        l1 = lf.astype(BF16)
        r1 = lf - l1.astype(F32)
        l2 = r1.astype(BF16)
        l3 = (r1 - l2.astype(F32)).astype(BF16)
        b = (_dot(tri, l3) + _dot(tri, l2)) + _dot(tri, l1)
        b_last = b[c_len - 1:c_len]

        st = st_ref[...]
        o = _dot_nt((q * jnp.exp(b)).astype(BF16), st.astype(BF16))
        ks = (kk * jnp.exp(b_last - b)).astype(BF16)
        st_ref[...] = jnp.exp(b_last) * st + _dot_tn(v, ks)

        attn = jnp.zeros((c_len, c_len), F32)
        for m, mask in zip(HG_LEVELS, level_masks):
            w = jnp.exp(-jnp.abs(b - _group_rows(b, m - 1, 2 * m, 2 * m)))
            attn = attn + jnp.where(mask, _dot_nt((q * w).astype(BF16), (kk * w).astype(BF16)), 0.0)
        d = b - _group_rows(b, 0, HG_DIAG, HG_DIAG)
        attn = attn + jnp.where(
            diag_mask, _dot_nt((q * jnp.exp(d)).astype(BF16), (kk * jnp.exp(-d)).astype(BF16)), 0.0)
        o = o + _dot(attn.astype(BF16), v)

        on = o * lax.rsqrt(jnp.mean(o * o, axis=-1, keepdims=True) + EPS) * gain
        o_ref[0, sl, :] = (on * g_ref[0, sl, :].astype(F32)).astype(o_ref.dtype)
        return carry

    lax.fori_loop(0, t_len // c_len, body, 0)


def _hgrn(hq, kk, lf, hv, hg, gain):
    bsz, t, hw = hq.shape
    d = hw // HG_HEADS
    blk = pl.BlockSpec((1, t, d), lambda b, h: (b, 0, h))
    return pl.pallas_call(
        _hgrn_kernel,
        out_shape=jax.ShapeDtypeStruct((bsz, t, hw), BF16),
        grid=(bsz, HG_HEADS),
        in_specs=[blk, blk, blk, blk, blk, pl.BlockSpec((1, d), lambda b, h: (0, 0))],
        out_specs=blk,
        scratch_shapes=[pltpu.VMEM((d, d), F32)],
        compiler_params=pltpu.CompilerParams(dimension_semantics=("parallel", "parallel")),
        name="hgrn",
    )(hq, kk, lf, hv, hg, gain.reshape(1, d))


def _moba_kernel(slope_ref, q_ref, k_ref, v_ref, gate_ref, o_ref, m_scr, l_scr, acc_scr):
    p = pl.program_id(1)
    n = pl.program_id(2)
    nb = k_ref.shape[1] // BLOCK
    dh = q_ref.shape[2] // 2

    q = q_ref[0]
    gate = gate_ref[0, 0]
    lane = lax.broadcasted_iota(jnp.int32, (1, 2 * dh), 1)
    r = lax.broadcasted_iota(jnp.int32, (BLOCK, BLOCK), 0)
    c = lax.broadcasted_iota(jnp.int32, (BLOCK, BLOCK), 1)
    dist = (r - c).astype(F32)
    causal = r >= c
    blk = lax.broadcasted_iota(jnp.int32, (BLOCK, nb), 1)
    k_own = k_ref[0, pl.ds(pl.multiple_of(n * BLOCK, BLOCK), BLOCK), :]
    v_own = v_ref[0, pl.ds(pl.multiple_of(n * BLOCK, BLOCK), BLOCK), :]

    outs = []
    for a in range(2):
        slope = slope_ref[2 * p + a]
        qa = jnp.where((lane >= a * dh) & (lane < (a + 1) * dh), q, jnp.zeros_like(q))

        g = jnp.where(blk < n, gate[:, a * nb:(a + 1) * nb], -jnp.inf)
        rank = jnp.zeros((BLOCK, nb), jnp.int32)
        for i in range(nb):
            gi = g[:, i:i + 1]
            rank = rank + ((gi > g) | ((gi == g) & (i < blk))).astype(jnp.int32)
        bias = jnp.where((rank < TOPK) & (blk < n), 0.0, NEG)

        s = _dot_nt(qa, k_own) - slope * dist
        s = jnp.where(causal, s, NEG)
        m0 = jnp.max(s, axis=-1, keepdims=True)
        pr = jnp.exp(s - m0)
        m_scr[a] = m0
        l_scr[a] = jnp.sum(pr, axis=-1, keepdims=True)
        acc_scr[a] = _dot(pr.astype(BF16), v_own)

        for j in range(nb - 1):
            @pl.when(j < n)
            def _(j=j, a=a, qa=qa, slope=slope, bias=bias):
                kj = k_ref[0, j * BLOCK:(j + 1) * BLOCK, :]
                vj = v_ref[0, j * BLOCK:(j + 1) * BLOCK, :]
                off = ((n - j) * BLOCK).astype(F32)
                sj = _dot_nt(qa, kj) - slope * (dist + off) + bias[:, j:j + 1]
                m_old = m_scr[a]
                m_new = jnp.maximum(m_old, jnp.max(sj, axis=-1, keepdims=True))
                alpha = jnp.exp(m_old - m_new)
                pj = jnp.exp(sj - m_new)
                m_scr[a] = m_new
                l_scr[a] = alpha * l_scr[a] + jnp.sum(pj, axis=-1, keepdims=True)
                acc_scr[a] = alpha * acc_scr[a] + _dot(pj.astype(BF16), vj)

        outs.append(acc_scr[a] / l_scr[a])

    o_ref[0] = jnp.where(lane < dh, outs[0], outs[1]).astype(o_ref.dtype)


def _moba(aq, ak, av, gate):
    bsz, t, aw = aq.shape
    dh = aw // ATT_HEADS
    nb = t // BLOCK
    slopes = jnp.exp2(-8.0 * jnp.arange(1, ATT_HEADS + 1, dtype=F32) / ATT_HEADS)
    return pl.pallas_call(
        _moba_kernel,
        out_shape=jax.ShapeDtypeStruct((bsz, t, aw), BF16),
        grid=(bsz, ATT_HEADS // 2, nb),
        in_specs=[pl.BlockSpec(memory_space=pltpu.SMEM),
                  pl.BlockSpec((1, BLOCK, 2 * dh), lambda b, p, n: (b, n, p)),
                  pl.BlockSpec((1, t, 2 * dh), lambda b, p, n: (b, 0, p)),
                  pl.BlockSpec((1, t, 2 * dh), lambda b, p, n: (b, 0, p)),
                  pl.BlockSpec((1, 1, BLOCK, 2 * nb), lambda b, p, n: (b, p, n, 0))],
        out_specs=pl.BlockSpec((1, BLOCK, 2 * dh), lambda b, p, n: (b, n, p)),
        scratch_shapes=[pltpu.VMEM((2, BLOCK, 1), F32), pltpu.VMEM((2, BLOCK, 1), F32),
                        pltpu.VMEM((2, BLOCK, 2 * dh), F32)],
        compiler_params=pltpu.CompilerParams(dimension_semantics=("parallel", "parallel", "arbitrary")),
        name="moba",
    )(slopes, aq, ak, av, gate)


def _ffn_kernel(x_ref, mh_ref, ma_ref, mod_ref, g2_ref, wo_ref, wg_ref, wu_ref, wd_ref, o_ref, *, ff_chunks):
    hw = mh_ref.shape[2]
    mod = mod_ref[0]
    mixed = _dot(mh_ref[0], wo_ref[0:hw]) + _dot(ma_ref[0], wo_ref[hw:])
    x1 = x_ref[0] + mod[2:3] * mixed
    xn = x1 * lax.rsqrt(jnp.mean(x1 * x1, axis=-1, keepdims=True) + EPS)
    hb = ((xn * g2_ref[...]) * (1.0 + mod[4:5]) + mod[3:4]).astype(BF16)
    fc = wg_ref.shape[1] // ff_chunks
    y = jnp.zeros_like(x1)
    for ci in range(ff_chunks):
        act = _silu(_dot(hb, wg_ref[:, ci * fc:(ci + 1) * fc])) * _dot(hb, wu_ref[:, ci * fc:(ci + 1) * fc])
        y = y + _dot(act.astype(BF16), wd_ref[ci * fc:(ci + 1) * fc])
    o_ref[0] = x1 + mod[5:6] * y


def _ffn(x, mh, ma, mod, g2, w_out, w_gate, w_up, w_down, *, tm=512, ff_chunks=2):
    bsz, t, d = x.shape
    tok = lambda w: pl.BlockSpec((1, tm, w), lambda b, i: (b, i, 0))
    const = lambda shape: pl.BlockSpec(shape, lambda b, i: (0,) * len(shape), pipeline_mode=pl.Buffered(1))
    return pl.pallas_call(
        functools.partial(_ffn_kernel, ff_chunks=ff_chunks),
        out_shape=jax.ShapeDtypeStruct((bsz, t, d), x.dtype),
        grid=(bsz, t // tm),
        in_specs=[tok(d), tok(mh.shape[2]), tok(ma.shape[2]),
                  pl.BlockSpec((1, 6, d), lambda b, i: (b, 0, 0)),
                  const((1, d)), const(w_out.shape), const(w_gate.shape), const(w_up.shape), const(w_down.shape)],
        out_specs=tok(d),
        compiler_params=pltpu.CompilerParams(
            dimension_semantics=("parallel", "parallel"), vmem_limit_bytes=VMEM_LIMIT),
        name="ffn",
    )(x, mh, ma, mod, g2.reshape(1, d), w_out.astype(BF16), w_gate.astype(BF16), w_up.astype(BF16),
      w_down.astype(BF16))


def kernel(x, c, w_ada, b_ada, norm1_g, w_in, lb_logits, hg_norm_g, q_norm_g, k_norm_g,
           w_out, norm2_g, w_gate, w_up, w_down):
    bsz, _, d = x.shape
    for l in range(w_ada.shape[0]):
        mod = _adaln(c, w_ada[l], b_ada[l]).reshape(bsz, 6, d)
        hq, kk, lf, hv, hg, aq, ak, av, gate = _inproj(
            x, mod, norm1_g[l], w_in[l], lb_logits, q_norm_g[l], k_norm_g[l], layer=l)
        o_hg = _hgrn(hq, kk, lf, hv, hg, hg_norm_g[l])
        o_att = _moba(aq, ak, av, gate)
        x = _ffn(x, o_hg, o_att, mod, norm2_g[l], w_out[l], w_gate[l], w_up[l], w_down[l])
    return x
```

```python
import functools

import jax
import jax.numpy as jnp
from jax import lax
from jax.experimental import pallas as pl
from jax.experimental.pallas import tpu as pltpu

F32 = jnp.float32
BF16 = jnp.bfloat16

HG_HEADS = 4
ATT_HEADS = 8
BLOCK = 256
TOPK = 3
HG_CHUNK = 64
EPS = 1e-6
NEG = -(2.0 ** 100)

VMEM_LIMIT = 56 * 1024 * 1024


def _silu(t):
    return t * jax.nn.sigmoid(t)


def _dot(a, b):
    return jnp.dot(a, b, preferred_element_type=F32)


def _dot_nt(a, b):
    return lax.dot_general(a, b, (((1,), (1,)), ((), ())), preferred_element_type=F32)


def _dot_tn(a, b):
    return lax.dot_general(a, b, (((0,), (0,)), ((), ())), preferred_element_type=F32)


def _split2(t):
    hi = t.astype(BF16)
    lo = (t - hi.astype(F32)).astype(BF16)
    return hi, lo


def _dot3(a, b, dot=_dot):
    a_hi, a_lo = _split2(a)
    b_hi, b_lo = _split2(b)
    return (dot(a_hi, b_lo) + dot(a_lo, b_hi)) + dot(a_hi, b_hi)


def _adaln_kernel(c_ref, w_ref, b_ref, o_ref):
    o_ref[...] = _dot3(_silu(c_ref[...]), w_ref[...]) + b_ref[...]


def _adaln(c, w, b):
    bsz, d = c.shape
    n = w.shape[1]
    tn = 1024
    return pl.pallas_call(
        _adaln_kernel,
        out_shape=jax.ShapeDtypeStruct((bsz, n), F32),
        grid=(n // tn,),
        in_specs=[pl.BlockSpec((bsz, d), lambda j: (0, 0)),
                  pl.BlockSpec((d, tn), lambda j: (0, j)),
                  pl.BlockSpec((1, tn), lambda j: (0, j))],
        out_specs=pl.BlockSpec((bsz, tn), lambda j: (0, j)),
        name="adaln",
    )(c, w, b.reshape(1, n))


def _inproj_kernel(x_ref, mod_ref, g1_ref, w_ref, lbl_ref, qg_ref, kg_ref, pool_ref,
                   hq_ref, kk_ref, lf_ref, hv_ref, hg_ref, aq_ref, ak_ref, av_ref, gate_ref,
                   kmean_scr, *, tm, hw, aw, layer):
    i = pl.program_id(1)
    dh = aw // ATT_HEADS
    nb = kmean_scr.shape[0]

    x = x_ref[0]
    xn = x * lax.rsqrt(jnp.mean(x * x, axis=-1, keepdims=True) + EPS)
    mod = mod_ref[0]
    h = (xn * g1_ref[...]) * (1.0 + mod[1:2]) + mod[0:1]
    hb = h.astype(BF16)

    def proj(lo, width):
        return _dot(hb, w_ref[:, lo:lo + width])

    hq_ref[0] = _silu(proj(0, hw)).astype(BF16)
    lbl = lbl_ref[...]
    e = jnp.exp(lbl - jnp.max(lbl, axis=0, keepdims=True))
    lb = jnp.sum(e[0:layer + 1], axis=0, keepdims=True) / jnp.sum(e, axis=0, keepdims=True)
    f = lb + (1.0 - lb) * jax.nn.sigmoid(proj(hw, hw))
    lf_ref[0] = jnp.log(f)
    kk_ref[0] = (1.0 - f).astype(BF16)
    hv_ref[0] = proj(2 * hw, hw).astype(BF16)
    hg_ref[0] = _silu(proj(3 * hw, hw)).astype(BF16)

    def head_norm(t, gain):
        ms = _dot((t * t).astype(BF16), pool_ref[...])
        return t * lax.rsqrt(ms + EPS) * gain

    qn = head_norm(proj(4 * hw, aw), qg_ref[...])
    kn = head_norm(proj(4 * hw + aw, aw), kg_ref[...])
    aq_ref[0] = (qn * dh ** -0.5).astype(BF16)
    ak_ref[0] = kn.astype(BF16)
    av_ref[0] = proj(4 * hw + 2 * aw, aw).astype(BF16)

    @pl.when(i == 0)
    def _():
        kmean_scr[...] = jnp.zeros_like(kmean_scr)

    per_tile = tm // BLOCK
    for s in range(per_tile):
        kmean_scr[pl.ds(i * per_tile + s, 1), :] = jnp.mean(
            kn[s * BLOCK:(s + 1) * BLOCK], axis=0, keepdims=True)

    kmean = kmean_scr[...]
    rows = lax.broadcasted_iota(jnp.int32, (ATT_HEADS * nb, aw), 0)
    cols = lax.broadcasted_iota(jnp.int32, (ATT_HEADS * nb, aw), 1)
    sel = jnp.where(rows // nb == cols // dh, jnp.concatenate([kmean] * ATT_HEADS, axis=0), 0.0)
    gate_ref[0] = _dot3(sel, qn, dot=_dot_nt)


def _inproj(x, mod, g1, w_in, lb_logits, qg, kg, *, layer, tm=512):
    bsz, t, d = x.shape
    hw = lb_logits.shape[1]
    aw = (w_in.shape[1] - 4 * hw) // 3
    dh = aw // ATT_HEADS
    nb = t // BLOCK
    pool = jnp.where(jnp.arange(aw)[:, None] // dh == jnp.arange(aw)[None, :] // dh, 1.0 / dh, 0.0).astype(BF16)
    tok = lambda w: pl.BlockSpec((1, tm, w), lambda b, i: (b, i, 0))
    const = lambda shape: pl.BlockSpec(shape, lambda b, i: (0,) * len(shape), pipeline_mode=pl.Buffered(1))
    bf = lambda w: jax.ShapeDtypeStruct((bsz, t, w), BF16)
    return pl.pallas_call(
        functools.partial(_inproj_kernel, tm=tm, hw=hw, aw=aw, layer=layer),
        out_shape=[bf(hw), bf(hw), jax.ShapeDtypeStruct((bsz, t, hw), F32), bf(hw), bf(hw),
                   bf(aw), bf(aw), bf(aw),
                   jax.ShapeDtypeStruct((bsz, ATT_HEADS * nb, t), F32)],
        grid=(bsz, t // tm),
        in_specs=[tok(d),
                  pl.BlockSpec((1, 6, d), lambda b, i: (b, 0, 0)),
                  const((1, d)), const(w_in.shape), const(lb_logits.shape),
                  const((1, aw)), const((1, aw)), const((aw, aw))],
        out_specs=[tok(hw), tok(hw), tok(hw), tok(hw), tok(hw), tok(aw), tok(aw), tok(aw),
                   pl.BlockSpec((1, ATT_HEADS * nb, tm), lambda b, i: (b, 0, i))],
        scratch_shapes=[pltpu.VMEM((nb, aw), F32)],
        compiler_params=pltpu.CompilerParams(
            dimension_semantics=("parallel", "arbitrary"), vmem_limit_bytes=VMEM_LIMIT),
        name="inproj",
    )(x, mod, g1.reshape(1, d), w_in.astype(BF16), lb_logits,
      jnp.tile(qg, ATT_HEADS).reshape(1, aw), jnp.tile(kg, ATT_HEADS).reshape(1, aw), pool)


HG_LEVELS = (32, 16, 8)
HG_DIAG = 8


def _group_rows(b, first, step, rows):
    n = b.shape[0] // rows
    return jnp.concatenate(
        [jnp.broadcast_to(b[first + g * step:first + g * step + 1], (rows, b.shape[1])) for g in range(n)], axis=0)


def _hgrn_kernel(q_ref, k_ref, lf_ref, v_ref, g_ref, gain_ref, o_ref, st_ref):
    c_len = HG_CHUNK
    t_len, hw = q_ref.shape[1], q_ref.shape[2]
    d = hw // HG_HEADS
    r = lax.broadcasted_iota(jnp.int32, (c_len, c_len), 0)
    c = lax.broadcasted_iota(jnp.int32, (c_len, c_len), 1)
    tri = (r >= c).astype(BF16)
    level_masks = [(r // (2 * m) == c // (2 * m)) & ((r // m) % 2 == 1) & ((c // m) % 2 == 0) for m in HG_LEVELS]
    diag_mask = (r // HG_DIAG == c // HG_DIAG) & (r >= c)
    gain = gain_ref[...]

    st_ref[...] = jnp.zeros_like(st_ref)

    def body(ci, carry):
        sl = pl.ds(pl.multiple_of(ci * c_len, c_len), c_len)
        q = q_ref[0, sl, :].astype(F32)
        kk = k_ref[0, sl, :].astype(F32)
        lf = lf_ref[0, sl, :]
        v = v_ref[0, sl, :]

        l1 = lf.astype(BF16)
        r1 = lf - l1.astype(F32)
        l2 = r1.astype(BF16)
        l3 = (r1 - l2.astype(F32)).astype(BF16)
        b = (_dot(tri, l3) + _dot(tri, l2)) + _dot(tri, l1)
        b_last = b[c_len - 1:c_len]

        q_in = (q * jnp.exp(b)).astype(BF16)
        k_st = (kk * jnp.exp(b_last - b)).astype(BF16)
        decay = jnp.exp(b_last)
        pairs = []
        for m in HG_LEVELS:
            w = jnp.exp(-jnp.abs(b - _group_rows(b, m - 1, 2 * m, 2 * m)))
            pairs.append(((q * w).astype(BF16), (kk * w).astype(BF16)))
        dd = b - _group_rows(b, 0, HG_DIAG, HG_DIAG)
        pairs.append(((q * jnp.exp(dd)).astype(BF16), (kk * jnp.exp(-dd)).astype(BF16)))

        outs = []
        for h in range(HG_HEADS):
            hs = slice(h * d, (h + 1) * d)
            st = st_ref[h]
            o = _dot_nt(q_in[:, hs], st.astype(BF16))
            st_ref[h] = decay[:, hs] * st + _dot_tn(v[:, hs], k_st[:, hs])
            attn = jnp.zeros((c_len, c_len), F32)
            for (qw, kw), mask in zip(pairs, level_masks + [diag_mask]):
                attn = attn + jnp.where(mask, _dot_nt(qw[:, hs], kw[:, hs]), 0.0)
            o = o + _dot(attn.astype(BF16), v[:, hs])
            outs.append(o * lax.rsqrt(jnp.mean(o * o, axis=-1, keepdims=True) + EPS) * gain)
        o_ref[0, sl, :] = (jnp.concatenate(outs, axis=1) * g_ref[0, sl, :].astype(F32)).astype(o_ref.dtype)
        return carry

    lax.fori_loop(0, t_len // c_len, body, 0)


def _hgrn(hq, kk, lf, hv, hg, gain):
    bsz, t, hw = hq.shape
    d = hw // HG_HEADS
    blk = pl.BlockSpec((1, t, hw), lambda b: (b, 0, 0))
    return pl.pallas_call(
        _hgrn_kernel,
        out_shape=jax.ShapeDtypeStruct((bsz, t, hw), BF16),
        grid=(bsz,),
        in_specs=[blk, blk, blk, blk, blk, pl.BlockSpec((1, d), lambda b: (0, 0))],
        out_specs=blk,
        scratch_shapes=[pltpu.VMEM((HG_HEADS, d, d), F32)],
        compiler_params=pltpu.CompilerParams(dimension_semantics=("parallel",), vmem_limit_bytes=VMEM_LIMIT),
        name="hgrn",
    )(hq, kk, lf, hv, hg, gain.reshape(1, d))


SLOPE_PARTS = 3
AUG_ROWS = 8


def _moba_kernel(slope_ref, q_ref, k_ref, v_ref, gate_ref, o_ref, kaug_scr, vaug_scr):
    p = pl.program_id(1)
    n_dyn = pl.program_id(2)
    t_len = k_ref.shape[1]
    nb = t_len // BLOCK
    width = q_ref.shape[2]
    dh = width // 2
    lane = lax.broadcasted_iota(jnp.int32, (1, width), 1)

    @pl.when(n_dyn == 0)
    def _():
        k = k_ref[0]
        v = v_ref[0]
        row = lax.broadcasted_iota(jnp.int32, k.shape, 0)
        ln = lax.broadcasted_iota(jnp.int32, k.shape, 1)
        in_block = (row % BLOCK).astype(F32)
        block_start = (row - row % BLOCK).astype(F32)
        for a in range(2):
            x0 = (1 - a) * dh
            y0 = x0 + AUG_ROWS
            extra = jnp.where(ln < x0 + SLOPE_PARTS, in_block,
                              jnp.where(ln < x0 + 2 * SLOPE_PARTS, block_start,
                                        (ln - y0 == row // BLOCK).astype(F32)))
            kaug_scr[a] = jnp.where((ln >= x0) & (ln < y0 + nb), extra.astype(BF16), k)
            vaug_scr[a] = jnp.where(ln == x0, jnp.ones_like(v), v)

    r = lax.broadcasted_iota(jnp.int32, (BLOCK, BLOCK), 0)
    c = lax.broadcasted_iota(jnp.int32, (BLOCK, BLOCK), 1)
    causal = r >= c
    jrow = lax.broadcasted_iota(jnp.int32, (nb, BLOCK), 0)
    crow = lax.broadcasted_iota(jnp.int32, (AUG_ROWS, BLOCK), 0)

    def head(n, a):
        x0 = (1 - a) * dh
        g = jnp.where(jrow < n, gate_ref[0, a * nb:(a + 1) * nb, :], -jnp.inf)
        rank = jnp.zeros((nb, BLOCK), jnp.int32)
        for i in range(n):
            gi = g[i:i + 1, :]
            rank = rank + ((gi > g) | ((gi == g) & (i < jrow))).astype(jnp.int32)
        sel_t = jnp.where((rank < TOPK) | (jrow >= n), 0.0, NEG)

        slope_t = jnp.zeros((AUG_ROWS, BLOCK), F32)
        for i in range(SLOPE_PARTS):
            slope_t = jnp.where((crow == i) | (crow == SLOPE_PARTS + i), slope_ref[i, 2 * p + a], slope_t)
        pieces = [slope_t, sel_t, jnp.zeros((width - x0 - AUG_ROWS - nb, BLOCK), F32)]
        if x0:
            pieces.insert(0, jnp.zeros((x0, BLOCK), F32))
        aug = jnp.concatenate(pieces, axis=0).T
        own = (lane >= a * dh) & (lane < (a + 1) * dh)
        qa = jnp.where(own, q_ref[0].astype(F32), aug).astype(BF16)

        kv_len = (n + 1) * BLOCK
        s = _dot_nt(qa, kaug_scr[a, 0:kv_len, :])
        s_own = jnp.where(causal, s[:, n * BLOCK:], NEG)
        s = s_own if n == 0 else jnp.concatenate([s[:, :n * BLOCK], s_own], axis=1)
        m = jnp.max(s, axis=-1, keepdims=True)
        acc = _dot(jnp.exp(s - m).astype(BF16), vaug_scr[a, 0:kv_len, :])
        return acc / acc[:, x0:x0 + 1]

    for n in range(nb):
        @pl.when(n_dyn == n)
        def _(n=n):
            o_ref[0] = jnp.where(lane < dh, head(n, 0), head(n, 1)).astype(o_ref.dtype)


def _moba(aq, ak, av, gate_t):
    bsz, t, aw = aq.shape
    dh = aw // ATT_HEADS
    nb = t // BLOCK
    assert nb % 8 == 0 and AUG_ROWS + nb <= dh and 2 * SLOPE_PARTS <= AUG_ROWS
    slopes = jnp.exp2(-8.0 * jnp.arange(1, ATT_HEADS + 1, dtype=F32) / ATT_HEADS)
    parts = []
    for _ in range(SLOPE_PARTS):
        parts.append(slopes.astype(BF16).astype(F32))
        slopes = slopes - parts[-1]
    kv_spec = pl.BlockSpec((1, t, 2 * dh), lambda b, p, n: (b, 0, p))
    return pl.pallas_call(
        _moba_kernel,
        out_shape=jax.ShapeDtypeStruct((bsz, t, aw), BF16),
        grid=(bsz, ATT_HEADS // 2, nb),
        in_specs=[pl.BlockSpec(memory_space=pltpu.SMEM),
                  pl.BlockSpec((1, BLOCK, 2 * dh), lambda b, p, n: (b, n, p)),
                  kv_spec, kv_spec,
                  pl.BlockSpec((1, 2 * nb, BLOCK), lambda b, p, n: (b, p, n))],
        out_specs=pl.BlockSpec((1, BLOCK, 2 * dh), lambda b, p, n: (b, n, p)),
        scratch_shapes=[pltpu.VMEM((2, t, 2 * dh), BF16), pltpu.VMEM((2, t, 2 * dh), BF16)],
        compiler_params=pltpu.CompilerParams(
            dimension_semantics=("parallel", "parallel", "arbitrary"), vmem_limit_bytes=VMEM_LIMIT),
        name="moba",
    )(jnp.stack(parts), aq, ak, av, gate_t)


def _ffn_kernel(x_ref, mh_ref, ma_ref, mod_ref, g2_ref, wo_ref, wg_ref, wu_ref, wd_ref, o_ref, *, ff_chunks):
    hw = mh_ref.shape[2]
    mod = mod_ref[0]
    mixed = _dot(mh_ref[0], wo_ref[0:hw]) + _dot(ma_ref[0], wo_ref[hw:])
    x1 = x_ref[0] + mod[2:3] * mixed
    xn = x1 * lax.rsqrt(jnp.mean(x1 * x1, axis=-1, keepdims=True) + EPS)
    hb = ((xn * g2_ref[...]) * (1.0 + mod[4:5]) + mod[3:4]).astype(BF16)
    fc = wg_ref.shape[1] // ff_chunks
    y = jnp.zeros_like(x1)
    for ci in range(ff_chunks):
        act = _silu(_dot(hb, wg_ref[:, ci * fc:(ci + 1) * fc])) * _dot(hb, wu_ref[:, ci * fc:(ci + 1) * fc])
        y = y + _dot(act.astype(BF16), wd_ref[ci * fc:(ci + 1) * fc])
    o_ref[0] = x1 + mod[5:6] * y


def _ffn(x, mh, ma, mod, g2, w_out, w_gate, w_up, w_down, *, tm=512, ff_chunks=2):
    bsz, t, d = x.shape
    tok = lambda w: pl.BlockSpec((1, tm, w), lambda b, i: (b, i, 0))
    const = lambda shape: pl.BlockSpec(shape, lambda b, i: (0,) * len(shape), pipeline_mode=pl.Buffered(1))
    return pl.pallas_call(
        functools.partial(_ffn_kernel, ff_chunks=ff_chunks),
        out_shape=jax.ShapeDtypeStruct((bsz, t, d), x.dtype),
        grid=(bsz, t // tm),
        in_specs=[tok(d), tok(mh.shape[2]), tok(ma.shape[2]),
                  pl.BlockSpec((1, 6, d), lambda b, i: (b, 0, 0)),
                  const((1, d)), const(w_out.shape), const(w_gate.shape), const(w_up.shape), const(w_down.shape)],
        out_specs=tok(d),
        compiler_params=pltpu.CompilerParams(
            dimension_semantics=("parallel", "parallel"), vmem_limit_bytes=VMEM_LIMIT),
        name="ffn",
    )(x, mh, ma, mod, g2.reshape(1, d), w_out.astype(BF16), w_gate.astype(BF16), w_up.astype(BF16),
      w_down.astype(BF16))


def kernel(x, c, w_ada, b_ada, norm1_g, w_in, lb_logits, hg_norm_g, q_norm_g, k_norm_g,
           w_out, norm2_g, w_gate, w_up, w_down):
    bsz, _, d = x.shape
    for l in range(w_ada.shape[0]):
        mod = _adaln(c, w_ada[l], b_ada[l]).reshape(bsz, 6, d)
        hq, kk, lf, hv, hg, aq, ak, av, gate = _inproj(
            x, mod, norm1_g[l], w_in[l], lb_logits, q_norm_g[l], k_norm_g[l], layer=l)
        o_hg = _hgrn(hq, kk, lf, hv, hg, hg_norm_g[l])
        o_att = _moba(aq, ak, av, gate)
        x = _ffn(x, o_hg, o_att, mod, norm2_g[l], w_out[l], w_gate[l], w_up[l], w_down[l])
    return x
```

```python
import functools

import jax
import jax.numpy as jnp
from jax import lax
from jax.experimental import pallas as pl
from jax.experimental.pallas import tpu as pltpu

F32 = jnp.float32
BF16 = jnp.bfloat16

HG_HEADS = 4
ATT_HEADS = 8
BLOCK = 256
TOPK = 3
HG_CHUNK = 256
EPS = 1e-6
NEG = -(2.0 ** 100)

VMEM_LIMIT = 56 * 1024 * 1024


def _silu(t):
    return t * jax.nn.sigmoid(t)


def _dot(a, b):
    return jnp.dot(a, b, preferred_element_type=F32)


def _dot_nt(a, b):
    return lax.dot_general(a, b, (((1,), (1,)), ((), ())), preferred_element_type=F32)


def _dot_tn(a, b):
    return lax.dot_general(a, b, (((0,), (0,)), ((), ())), preferred_element_type=F32)


def _split2(t):
    hi = t.astype(BF16)
    lo = (t - hi.astype(F32)).astype(BF16)
    return hi, lo


def _dot3(a, b, dot=_dot):
    a_hi, a_lo = _split2(a)
    b_hi, b_lo = _split2(b)
    return (dot(a_hi, b_lo) + dot(a_lo, b_hi)) + dot(a_hi, b_hi)


def _adaln_kernel(c_ref, w_ref, b_ref, o_ref):
    o_ref[...] = _dot3(_silu(c_ref[...]), w_ref[...]) + b_ref[...]


def _adaln(c, w, b):
    bsz, d = c.shape
    n = w.shape[1]
    tn = 1024
    return pl.pallas_call(
        _adaln_kernel,
        out_shape=jax.ShapeDtypeStruct((bsz, n), F32),
        grid=(n // tn,),
        in_specs=[pl.BlockSpec((bsz, d), lambda j: (0, 0)),
                  pl.BlockSpec((d, tn), lambda j: (0, j)),
                  pl.BlockSpec((1, tn), lambda j: (0, j))],
        out_specs=pl.BlockSpec((bsz, tn), lambda j: (0, j)),
        name="adaln",
    )(c, w, b.reshape(1, n))


def _inproj_kernel(x_ref, mod_ref, g1_ref, w_ref, lbl_ref, qg_ref, kg_ref, pool_ref,
                   hq_ref, kk_ref, lf_ref, hv_ref, hg_ref, aq_ref, ak_ref, av_ref, gate_ref,
                   kmean_scr, *, tm, hw, aw, layer):
    i = pl.program_id(1)
    dh = aw // ATT_HEADS
    nb = kmean_scr.shape[0]

    x = x_ref[0]
    xn = x * lax.rsqrt(jnp.mean(x * x, axis=-1, keepdims=True) + EPS)
    mod = mod_ref[0]
    h = (xn * g1_ref[...]) * (1.0 + mod[1:2]) + mod[0:1]
    hb = h.astype(BF16)

    def proj(lo, width):
        return _dot(hb, w_ref[:, lo:lo + width])

    hq_ref[0] = _silu(proj(0, hw)).astype(BF16)
    lbl = lbl_ref[...]
    e = jnp.exp(lbl - jnp.max(lbl, axis=0, keepdims=True))
    lb = jnp.sum(e[0:layer + 1], axis=0, keepdims=True) / jnp.sum(e, axis=0, keepdims=True)
    f = lb + (1.0 - lb) * jax.nn.sigmoid(proj(hw, hw))
    lf_ref[0] = jnp.log2(f)
    kk_ref[0] = (1.0 - f).astype(BF16)
    hv_ref[0] = proj(2 * hw, hw).astype(BF16)
    hg_ref[0] = _silu(proj(3 * hw, hw)).astype(BF16)

    def head_norm(t, gain):
        ms = _dot((t * t).astype(BF16), pool_ref[...])
        return t * lax.rsqrt(ms + EPS) * gain

    qn = head_norm(proj(4 * hw, aw), qg_ref[...])
    kn = head_norm(proj(4 * hw + aw, aw), kg_ref[...])
    aq_ref[0] = (qn * dh ** -0.5).astype(BF16)
    ak_ref[0] = kn.astype(BF16)
    av_ref[0] = proj(4 * hw + 2 * aw, aw).astype(BF16)

    @pl.when(i == 0)
    def _():
        kmean_scr[...] = jnp.zeros_like(kmean_scr)

    per_tile = tm // BLOCK
    for s in range(per_tile):
        kmean_scr[pl.ds(i * per_tile + s, 1), :] = jnp.mean(
            kn[s * BLOCK:(s + 1) * BLOCK], axis=0, keepdims=True)

    kmean = kmean_scr[...]
    rows = lax.broadcasted_iota(jnp.int32, (ATT_HEADS * nb, aw), 0)
    cols = lax.broadcasted_iota(jnp.int32, (ATT_HEADS * nb, aw), 1)
    sel = jnp.where(rows // nb == cols // dh, jnp.concatenate([kmean] * ATT_HEADS, axis=0), 0.0)
    gate_ref[0] = _dot3(sel, qn, dot=_dot_nt)


def _inproj(x, mod, g1, w_in, lb_logits, qg, kg, *, layer, tm=512):
    bsz, t, d = x.shape
    hw = lb_logits.shape[1]
    aw = (w_in.shape[1] - 4 * hw) // 3
    dh = aw // ATT_HEADS
    nb = t // BLOCK
    pool = jnp.where(jnp.arange(aw)[:, None] // dh == jnp.arange(aw)[None, :] // dh, 1.0 / dh, 0.0).astype(BF16)
    tok = lambda w: pl.BlockSpec((1, tm, w), lambda b, i: (b, i, 0))
    const = lambda shape: pl.BlockSpec(shape, lambda b, i: (0,) * len(shape), pipeline_mode=pl.Buffered(1))
    bf = lambda w: jax.ShapeDtypeStruct((bsz, t, w), BF16)
    return pl.pallas_call(
        functools.partial(_inproj_kernel, tm=tm, hw=hw, aw=aw, layer=layer),
        out_shape=[bf(hw), bf(hw), jax.ShapeDtypeStruct((bsz, t, hw), F32), bf(hw), bf(hw),
                   bf(aw), bf(aw), bf(aw),
                   jax.ShapeDtypeStruct((bsz, ATT_HEADS * nb, t), F32)],
        grid=(bsz, t // tm),
        in_specs=[tok(d),
                  pl.BlockSpec((1, 6, d), lambda b, i: (b, 0, 0)),
                  const((1, d)), const(w_in.shape), const(lb_logits.shape),
                  const((1, aw)), const((1, aw)), const((aw, aw))],
        out_specs=[tok(hw), tok(hw), tok(hw), tok(hw), tok(hw), tok(aw), tok(aw), tok(aw),
                   pl.BlockSpec((1, ATT_HEADS * nb, tm), lambda b, i: (b, 0, i))],
        scratch_shapes=[pltpu.VMEM((nb, aw), F32)],
        compiler_params=pltpu.CompilerParams(
            dimension_semantics=("parallel", "arbitrary"), vmem_limit_bytes=VMEM_LIMIT),
        name="inproj",
    )(x, mod, g1.reshape(1, d), w_in.astype(BF16), lb_logits,
      jnp.tile(qg, ATT_HEADS).reshape(1, aw), jnp.tile(kg, ATT_HEADS).reshape(1, aw), pool)


HG_LEVELS = (128, 64, 32, 16, 8)
HG_DIAG = 8
HG_UNROLL = 1
BF16_ROWS = 16


def _group_rows(b, first, step, rows):
    n = b.shape[0] // rows
    return jnp.concatenate(
        [jnp.broadcast_to(b[first + g * step:first + g * step + 1], (rows, b.shape[1])) for g in range(n)], axis=0)


def _blend_rows(q, kk, m):
    n = q.shape[0] // m
    return jnp.concatenate([(q if g % 2 else kk)[g * m:(g + 1) * m] for g in range(n)], axis=0)


def _hgrn_kernel(q_ref, k_ref, lf_ref, v_ref, g_ref, gain_ref, o_ref, st_ref):
    c_len = HG_CHUNK
    t_len, hw = q_ref.shape[1], q_ref.shape[2]
    d = hw // HG_HEADS
    r = lax.broadcasted_iota(jnp.int32, (c_len, c_len), 0)
    c = lax.broadcasted_iota(jnp.int32, (c_len, c_len), 1)
    tri = (r >= c).astype(BF16)
    level_masks = [(r // (2 * m) == c // (2 * m)) & ((r // m) % 2 == 1) & ((c // m) % 2 == 0) for m in HG_LEVELS]
    diag_mask = (r // HG_DIAG == c // HG_DIAG) & (r >= c)
    gain = gain_ref[...]

    st_ref[...] = jnp.zeros_like(st_ref)

    def body(ci, carry):
        sl = pl.ds(pl.multiple_of(ci * c_len, c_len), c_len)
        q = q_ref[0, sl, :]
        kk = k_ref[0, sl, :]
        lf = lf_ref[0, sl, :]
        v = v_ref[0, sl, :]

        l1 = lf.astype(BF16)
        r1 = lf - l1.astype(F32)
        l2 = r1.astype(BF16)
        l3 = (r1 - l2.astype(F32)).astype(BF16)
        b = (_dot(tri, l3) + _dot(tri, l2)) + _dot(tri, l1)
        b_last = b[c_len - 1:c_len]

        q_in = q * jnp.exp2(b).astype(BF16)
        k_st = kk * jnp.exp2(b_last - b).astype(BF16)
        decay = jnp.exp2(b_last)
        pairs = []
        for m in HG_LEVELS:
            w = jnp.exp2(-jnp.abs(b - _group_rows(b, m - 1, 2 * m, 2 * m)))
            if m % BF16_ROWS == 0:
                x = _blend_rows(q, kk, m) * w.astype(BF16)
            else:
                x = (_blend_rows(q.astype(F32), kk.astype(F32), m) * w).astype(BF16)
            pairs.append((x, x))
        dd = b - _group_rows(b, 0, HG_DIAG, HG_DIAG)
        pairs.append((q * jnp.exp2(dd).astype(BF16), kk * jnp.exp2(-dd).astype(BF16)))

        outs = []
        for h in range(HG_HEADS):
            hs = slice(h * d, (h + 1) * d)
            st = st_ref[h]
            o = _dot_nt(q_in[:, hs], st.astype(BF16))
            st_ref[h] = decay[:, hs] * st + _dot_tn(v[:, hs], k_st[:, hs])
            attn = jnp.zeros((c_len, c_len), F32)
            for (qw, kw), mask in zip(pairs, level_masks + [diag_mask]):
                attn = jnp.where(mask, _dot_nt(qw[:, hs], kw[:, hs]), attn)
            o = o + _dot(attn.astype(BF16), v[:, hs])
            outs.append(o * lax.rsqrt(jnp.mean(o * o, axis=-1, keepdims=True) + EPS) * gain)
        o_ref[0, sl, :] = (jnp.concatenate(outs, axis=1) * g_ref[0, sl, :].astype(F32)).astype(o_ref.dtype)
        return carry

    lax.fori_loop(0, t_len // c_len, body, 0, unroll=HG_UNROLL)


def _hgrn(hq, kk, lf, hv, hg, gain):
    bsz, t, hw = hq.shape
    d = hw // HG_HEADS
    blk = pl.BlockSpec((1, t, hw), lambda b: (b, 0, 0))
    return pl.pallas_call(
        _hgrn_kernel,
        out_shape=jax.ShapeDtypeStruct((bsz, t, hw), BF16),
        grid=(bsz,),
        in_specs=[blk, blk, blk, blk, blk, pl.BlockSpec((1, d), lambda b: (0, 0))],
        out_specs=blk,
        scratch_shapes=[pltpu.VMEM((HG_HEADS, d, d), F32)],
        compiler_params=pltpu.CompilerParams(dimension_semantics=("parallel",), vmem_limit_bytes=VMEM_LIMIT),
        name="hgrn",
    )(hq, kk, lf, hv, hg, gain.reshape(1, d))


SLOPE_PARTS = 3
AUG_ROWS = 8


def _moba_kernel(slope_ref, q_ref, k_ref, v_ref, gate_ref, o_ref, kaug_scr, vaug_scr):
    p = pl.program_id(1)
    n_dyn = pl.program_id(2)
    t_len = k_ref.shape[1]
    nb = t_len // BLOCK
    width = q_ref.shape[2]
    dh = width // 2
    lane = lax.broadcasted_iota(jnp.int32, (1, width), 1)

    @pl.when(n_dyn == 0)
    def _():
        k = k_ref[0]
        v = v_ref[0]
        row = lax.broadcasted_iota(jnp.int32, k.shape, 0)
        ln = lax.broadcasted_iota(jnp.int32, k.shape, 1)
        in_block = (row % BLOCK).astype(F32)
        block_start = (row - row % BLOCK).astype(F32)
        for a in range(2):
            x0 = (1 - a) * dh
            y0 = x0 + AUG_ROWS
            extra = jnp.where(ln < x0 + SLOPE_PARTS, in_block,
                              jnp.where(ln < x0 + 2 * SLOPE_PARTS, block_start,
                                        (ln - y0 == row // BLOCK).astype(F32)))
            kaug_scr[a] = jnp.where((ln >= x0) & (ln < y0 + nb), extra.astype(BF16), k)
            vaug_scr[a] = jnp.where(ln == x0, jnp.ones_like(v), v)

    r = lax.broadcasted_iota(jnp.int32, (BLOCK, BLOCK), 0)
    c = lax.broadcasted_iota(jnp.int32, (BLOCK, BLOCK), 1)
    causal = r >= c
    jrow = lax.broadcasted_iota(jnp.int32, (nb, BLOCK), 0)
    crow = lax.broadcasted_iota(jnp.int32, (AUG_ROWS, BLOCK), 0)

    def head(n, a):
        x0 = (1 - a) * dh
        g = jnp.where(jrow < n, gate_ref[0, a * nb:(a + 1) * nb, :], -jnp.inf)
        rank = jnp.zeros((nb, BLOCK), jnp.int32)
        for i in range(n):
            gi = g[i:i + 1, :]
            rank = rank + ((gi > g) | ((gi == g) & (i < jrow))).astype(jnp.int32)
        sel_t = jnp.where((rank < TOPK) | (jrow >= n), 0.0, NEG)

        slope_t = jnp.zeros((AUG_ROWS, BLOCK), F32)
        for i in range(SLOPE_PARTS):
            slope_t = jnp.where((crow == i) | (crow == SLOPE_PARTS + i), slope_ref[i, 2 * p + a], slope_t)
        pieces = [slope_t, sel_t, jnp.zeros((width - x0 - AUG_ROWS - nb, BLOCK), F32)]
        if x0:
            pieces.insert(0, jnp.zeros((x0, BLOCK), F32))
        aug = jnp.concatenate(pieces, axis=0).T
        own = (lane >= a * dh) & (lane < (a + 1) * dh)
        qa = jnp.where(own, q_ref[0].astype(F32), aug).astype(BF16)

        kv_len = (n + 1) * BLOCK
        s = _dot_nt(qa, kaug_scr[a, 0:kv_len, :])
        s_own = jnp.where(causal, s[:, n * BLOCK:], NEG)
        s = s_own if n == 0 else jnp.concatenate([s[:, :n * BLOCK], s_own], axis=1)
        m = jnp.max(s, axis=-1, keepdims=True)
        acc = _dot(jnp.exp(s - m).astype(BF16), vaug_scr[a, 0:kv_len, :])
        return acc / acc[:, x0:x0 + 1]

    for n in range(nb):
        @pl.when(n_dyn == n)
        def _(n=n):
            o_ref[0] = jnp.where(lane < dh, head(n, 0), head(n, 1)).astype(o_ref.dtype)


def _moba(aq, ak, av, gate_t):
    bsz, t, aw = aq.shape
    dh = aw // ATT_HEADS
    nb = t // BLOCK
    assert nb % 8 == 0 and AUG_ROWS + nb <= dh and 2 * SLOPE_PARTS <= AUG_ROWS
    slopes = jnp.exp2(-8.0 * jnp.arange(1, ATT_HEADS + 1, dtype=F32) / ATT_HEADS)
    parts = []
    for _ in range(SLOPE_PARTS):
        parts.append(slopes.astype(BF16).astype(F32))
        slopes = slopes - parts[-1]
    kv_spec = pl.BlockSpec((1, t, 2 * dh), lambda b, p, n: (b, 0, p))
    return pl.pallas_call(
        _moba_kernel,
        out_shape=jax.ShapeDtypeStruct((bsz, t, aw), BF16),
        grid=(bsz, ATT_HEADS // 2, nb),
        in_specs=[pl.BlockSpec(memory_space=pltpu.SMEM),
                  pl.BlockSpec((1, BLOCK, 2 * dh), lambda b, p, n: (b, n, p)),
                  kv_spec, kv_spec,
                  pl.BlockSpec((1, 2 * nb, BLOCK), lambda b, p, n: (b, p, n))],
        out_specs=pl.BlockSpec((1, BLOCK, 2 * dh), lambda b, p, n: (b, n, p)),
        scratch_shapes=[pltpu.VMEM((2, t, 2 * dh), BF16), pltpu.VMEM((2, t, 2 * dh), BF16)],
        compiler_params=pltpu.CompilerParams(
            dimension_semantics=("parallel", "parallel", "arbitrary"), vmem_limit_bytes=VMEM_LIMIT),
        name="moba",
    )(jnp.stack(parts), aq, ak, av, gate_t)


def _ffn_kernel(x_ref, mh_ref, ma_ref, mod_ref, g2_ref, wo_ref, wg_ref, wu_ref, wd_ref, o_ref, *, ff_chunks):
    hw = mh_ref.shape[2]
    mod = mod_ref[0]
    mixed = _dot(mh_ref[0], wo_ref[0:hw]) + _dot(ma_ref[0], wo_ref[hw:])
    x1 = x_ref[0] + mod[2:3] * mixed
    xn = x1 * lax.rsqrt(jnp.mean(x1 * x1, axis=-1, keepdims=True) + EPS)
    hb = ((xn * g2_ref[...]) * (1.0 + mod[4:5]) + mod[3:4]).astype(BF16)
    fc = wg_ref.shape[1] // ff_chunks
    y = jnp.zeros_like(x1)
    for ci in range(ff_chunks):
        act = _silu(_dot(hb, wg_ref[:, ci * fc:(ci + 1) * fc])) * _dot(hb, wu_ref[:, ci * fc:(ci + 1) * fc])
        y = y + _dot(act.astype(BF16), wd_ref[ci * fc:(ci + 1) * fc])
    o_ref[0] = x1 + mod[5:6] * y


def _ffn(x, mh, ma, mod, g2, w_out, w_gate, w_up, w_down, *, tm=512, ff_chunks=2):
    bsz, t, d = x.shape
    tok = lambda w: pl.BlockSpec((1, tm, w), lambda b, i: (b, i, 0))
    const = lambda shape: pl.BlockSpec(shape, lambda b, i: (0,) * len(shape), pipeline_mode=pl.Buffered(1))
    return pl.pallas_call(
        functools.partial(_ffn_kernel, ff_chunks=ff_chunks),
        out_shape=jax.ShapeDtypeStruct((bsz, t, d), x.dtype),
        grid=(bsz, t // tm),
        in_specs=[tok(d), tok(mh.shape[2]), tok(ma.shape[2]),
                  pl.BlockSpec((1, 6, d), lambda b, i: (b, 0, 0)),
                  const((1, d)), const(w_out.shape), const(w_gate.shape), const(w_up.shape), const(w_down.shape)],
        out_specs=tok(d),
        compiler_params=pltpu.CompilerParams(
            dimension_semantics=("parallel", "parallel"), vmem_limit_bytes=VMEM_LIMIT),
        name="ffn",
    )(x, mh, ma, mod, g2.reshape(1, d), w_out.astype(BF16), w_gate.astype(BF16), w_up.astype(BF16),
      w_down.astype(BF16))


def kernel(x, c, w_ada, b_ada, norm1_g, w_in, lb_logits, hg_norm_g, q_norm_g, k_norm_g,
           w_out, norm2_g, w_gate, w_up, w_down):
    bsz, _, d = x.shape
    for l in range(w_ada.shape[0]):
        mod = _adaln(c, w_ada[l], b_ada[l]).reshape(bsz, 6, d)
        hq, kk, lf, hv, hg, aq, ak, av, gate = _inproj(
            x, mod, norm1_g[l], w_in[l], lb_logits, q_norm_g[l], k_norm_g[l], layer=l)
        o_hg = _hgrn(hq, kk, lf, hv, hg, hg_norm_g[l])
        o_att = _moba(aq, ak, av, gate)
        x = _ffn(x, o_hg, o_att, mod, norm2_g[l], w_out[l], w_gate[l], w_up[l], w_down[l])
    return x
```

```python
import functools

import jax
import jax.numpy as jnp
from jax import lax
from jax.experimental import pallas as pl
from jax.experimental.pallas import tpu as pltpu

F32 = jnp.float32
BF16 = jnp.bfloat16

HG_HEADS = 4
ATT_HEADS = 8
BLOCK = 256
TOPK = 3
HG_CHUNK = 256
EPS = 1e-6
NEG = -(2.0 ** 100)

VMEM_LIMIT = 56 * 1024 * 1024


def _silu(t):
    return t * jax.nn.sigmoid(t)


def _dot(a, b):
    return jnp.dot(a, b, preferred_element_type=F32)


def _dot_nt(a, b):
    return lax.dot_general(a, b, (((1,), (1,)), ((), ())), preferred_element_type=F32)


def _dot_tn(a, b):
    return lax.dot_general(a, b, (((0,), (0,)), ((), ())), preferred_element_type=F32)


def _split2(t):
    hi = t.astype(BF16)
    lo = (t - hi.astype(F32)).astype(BF16)
    return hi, lo


def _dot3(a, b, dot=_dot):
    a_hi, a_lo = _split2(a)
    b_hi, b_lo = _split2(b)
    return (dot(a_hi, b_lo) + dot(a_lo, b_hi)) + dot(a_hi, b_hi)


def _adaln_kernel(c_ref, w_ref, b_ref, o_ref):
    o_ref[...] = _dot3(_silu(c_ref[...]), w_ref[...]) + b_ref[...]


def _adaln(c, w, b):
    bsz, d = c.shape
    n = w.shape[1]
    tn = 1024
    return pl.pallas_call(
        _adaln_kernel,
        out_shape=jax.ShapeDtypeStruct((bsz, n), F32),
        grid=(n // tn,),
        in_specs=[pl.BlockSpec((bsz, d), lambda j: (0, 0)),
                  pl.BlockSpec((d, tn), lambda j: (0, j)),
                  pl.BlockSpec((1, tn), lambda j: (0, j))],
        out_specs=pl.BlockSpec((bsz, tn), lambda j: (0, j)),
        name="adaln",
    )(c, w, b.reshape(1, n))


def _inproj_kernel(x_ref, mod_ref, g1_ref, w_ref, lbl_ref, qg_ref, kg_ref, pool_ref,
                   hq_ref, kk_ref, lf_ref, hv_ref, hg_ref, aq_ref, ak_ref, av_ref, gate_ref,
                   kmean_scr, *, tm, hw, aw, layer):
    i = pl.program_id(1)
    dh = aw // ATT_HEADS
    nb = kmean_scr.shape[0]

    x = x_ref[0]
    xn = x * lax.rsqrt(jnp.mean(x * x, axis=-1, keepdims=True) + EPS)
    mod = mod_ref[0]
    h = (xn * g1_ref[...]) * (1.0 + mod[1:2]) + mod[0:1]
    hb = h.astype(BF16)

    def proj(lo, width):
        return _dot(hb, w_ref[:, lo:lo + width])

    hq_ref[0] = _silu(proj(0, hw)).astype(BF16)
    lbl = lbl_ref[...]
    e = jnp.exp(lbl - jnp.max(lbl, axis=0, keepdims=True))
    lb = jnp.sum(e[0:layer + 1], axis=0, keepdims=True) / jnp.sum(e, axis=0, keepdims=True)
    f = lb + (1.0 - lb) * jax.nn.sigmoid(proj(hw, hw))
    lf_ref[0] = jnp.log2(f)
    kk_ref[0] = (1.0 - f).astype(BF16)
    hv_ref[0] = proj(2 * hw, hw).astype(BF16)
    hg_ref[0] = _silu(proj(3 * hw, hw)).astype(BF16)

    def head_norm(t, gain):
        ms = _dot((t * t).astype(BF16), pool_ref[...])
        return t * lax.rsqrt(ms + EPS) * gain

    qn = head_norm(proj(4 * hw, aw), qg_ref[...])
    kn = head_norm(proj(4 * hw + aw, aw), kg_ref[...])
    aq_ref[0] = (qn * dh ** -0.5).astype(BF16)
    ak_ref[0] = kn.astype(BF16)
    av_ref[0] = proj(4 * hw + 2 * aw, aw).astype(BF16)

    @pl.when(i == 0)
    def _():
        kmean_scr[...] = jnp.zeros_like(kmean_scr)

    per_tile = tm // BLOCK
    for s in range(per_tile):
        kmean_scr[pl.ds(i * per_tile + s, 1), :] = jnp.mean(
            kn[s * BLOCK:(s + 1) * BLOCK], axis=0, keepdims=True)

    kmean = kmean_scr[...]
    rows = lax.broadcasted_iota(jnp.int32, (ATT_HEADS * nb, aw), 0)
    cols = lax.broadcasted_iota(jnp.int32, (ATT_HEADS * nb, aw), 1)
    sel = jnp.where(rows // nb == cols // dh, jnp.concatenate([kmean] * ATT_HEADS, axis=0), 0.0)
    gate_ref[0] = _dot3(sel, qn, dot=_dot_nt)


def _inproj(x, mod, g1, w_in, lb_logits, qg, kg, *, layer, tm=512):
    bsz, t, d = x.shape
    hw = lb_logits.shape[1]
    aw = (w_in.shape[1] - 4 * hw) // 3
    dh = aw // ATT_HEADS
    nb = t // BLOCK
    pool = jnp.where(jnp.arange(aw)[:, None] // dh == jnp.arange(aw)[None, :] // dh, 1.0 / dh, 0.0).astype(BF16)
    tok = lambda w: pl.BlockSpec((1, tm, w), lambda b, i: (b, i, 0))
    const = lambda shape: pl.BlockSpec(shape, lambda b, i: (0,) * len(shape), pipeline_mode=pl.Buffered(1))
    bf = lambda w: jax.ShapeDtypeStruct((bsz, t, w), BF16)
    return pl.pallas_call(
        functools.partial(_inproj_kernel, tm=tm, hw=hw, aw=aw, layer=layer),
        out_shape=[bf(hw), bf(hw), jax.ShapeDtypeStruct((bsz, t, hw), F32), bf(hw), bf(hw),
                   bf(aw), bf(aw), bf(aw),
                   jax.ShapeDtypeStruct((bsz, ATT_HEADS * nb, t), F32)],
        grid=(bsz, t // tm),
        in_specs=[tok(d),
                  pl.BlockSpec((1, 6, d), lambda b, i: (b, 0, 0)),
                  const((1, d)), const(w_in.shape), const(lb_logits.shape),
                  const((1, aw)), const((1, aw)), const((aw, aw))],
        out_specs=[tok(hw), tok(hw), tok(hw), tok(hw), tok(hw), tok(aw), tok(aw), tok(aw),
                   pl.BlockSpec((1, ATT_HEADS * nb, tm), lambda b, i: (b, 0, i))],
        scratch_shapes=[pltpu.VMEM((nb, aw), F32)],
        compiler_params=pltpu.CompilerParams(
            dimension_semantics=("parallel", "arbitrary"), vmem_limit_bytes=VMEM_LIMIT),
        name="inproj",
    )(x, mod, g1.reshape(1, d), w_in.astype(BF16), lb_logits,
      jnp.tile(qg, ATT_HEADS).reshape(1, aw), jnp.tile(kg, ATT_HEADS).reshape(1, aw), pool)


HG_LEVELS = (128, 64, 32, 16, 8)
HG_DIAG = 8
HG_UNROLL = 1
BF16_ROWS = 16


def _group_rows(b, first, step, rows):
    n = b.shape[0] // rows
    return jnp.concatenate(
        [jnp.broadcast_to(b[first + g * step:first + g * step + 1], (rows, b.shape[1])) for g in range(n)], axis=0)


def _blend_rows(q, kk, m):
    n = q.shape[0] // m
    return jnp.concatenate([(q if g % 2 else kk)[g * m:(g + 1) * m] for g in range(n)], axis=0)


def _hgrn_kernel(q_ref, k_ref, lf_ref, v_ref, g_ref, gain_ref, o_ref, st_ref):
    c_len = HG_CHUNK
    t_len, hw = q_ref.shape[1], q_ref.shape[2]
    d = hw // HG_HEADS
    r = lax.broadcasted_iota(jnp.int32, (c_len, c_len), 0)
    c = lax.broadcasted_iota(jnp.int32, (c_len, c_len), 1)
    tri = (r >= c).astype(BF16)
    level_masks = [(r // (2 * m) == c // (2 * m)) & ((r // m) % 2 == 1) & ((c // m) % 2 == 0) for m in HG_LEVELS]
    diag_mask = (r // HG_DIAG == c // HG_DIAG) & (r >= c)
    gain = gain_ref[...]

    st_ref[...] = jnp.zeros_like(st_ref)

    def body(ci, carry):
        sl = pl.ds(pl.multiple_of(ci * c_len, c_len), c_len)
        q = q_ref[0, sl, :]
        kk = k_ref[0, sl, :]
        lf = lf_ref[0, sl, :]
        v = v_ref[0, sl, :]

        l1 = lf.astype(BF16)
        r1 = lf - l1.astype(F32)
        l2 = r1.astype(BF16)
        l3 = (r1 - l2.astype(F32)).astype(BF16)
        b = (_dot(tri, l3) + _dot(tri, l2)) + _dot(tri, l1)
        b_last = b[c_len - 1:c_len]

        q_in = q * jnp.exp2(b).astype(BF16)
        k_st = kk * jnp.exp2(b_last - b).astype(BF16)
        decay = jnp.exp2(b_last)
        pairs = []
        for m in HG_LEVELS:
            w = jnp.exp2(-jnp.abs(b - _group_rows(b, m - 1, 2 * m, 2 * m)))
            if m % BF16_ROWS == 0:
                x = _blend_rows(q, kk, m) * w.astype(BF16)
            else:
                x = (_blend_rows(q.astype(F32), kk.astype(F32), m) * w).astype(BF16)
            pairs.append((x, x))
        dd = b - _group_rows(b, 0, HG_DIAG, HG_DIAG)
        pairs.append((q * jnp.exp2(dd).astype(BF16), kk * jnp.exp2(-dd).astype(BF16)))

        outs = []
        for h in range(HG_HEADS):
            hs = slice(h * d, (h + 1) * d)
            st = st_ref[h]
            o = _dot_nt(q_in[:, hs], st.astype(BF16))
            st_ref[h] = decay[:, hs] * st + _dot_tn(v[:, hs], k_st[:, hs])
            attn = jnp.zeros((c_len, c_len), F32)
            for (qw, kw), mask in zip(pairs, level_masks + [diag_mask]):
                attn = jnp.where(mask, _dot_nt(qw[:, hs], kw[:, hs]), attn)
            o = o + _dot(attn.astype(BF16), v[:, hs])
            outs.append(o * lax.rsqrt(jnp.mean(o * o, axis=-1, keepdims=True) + EPS) * gain)
        o_ref[0, sl, :] = (jnp.concatenate(outs, axis=1) * g_ref[0, sl, :].astype(F32)).astype(o_ref.dtype)
        return carry

    lax.fori_loop(0, t_len // c_len, body, 0, unroll=HG_UNROLL)


def _hgrn(hq, kk, lf, hv, hg, gain):
    bsz, t, hw = hq.shape
    d = hw // HG_HEADS
    blk = pl.BlockSpec((1, t, hw), lambda b: (b, 0, 0))
    return pl.pallas_call(
        _hgrn_kernel,
        out_shape=jax.ShapeDtypeStruct((bsz, t, hw), BF16),
        grid=(bsz,),
        in_specs=[blk, blk, blk, blk, blk, pl.BlockSpec((1, d), lambda b: (0, 0))],
        out_specs=blk,
        scratch_shapes=[pltpu.VMEM((HG_HEADS, d, d), F32)],
        compiler_params=pltpu.CompilerParams(dimension_semantics=("parallel",), vmem_limit_bytes=VMEM_LIMIT),
        name="hgrn",
    )(hq, kk, lf, hv, hg, gain.reshape(1, d))


SLOPE_PARTS = 3
AUG_ROWS = 8


def _moba_kernel(slope_ref, q_ref, k_ref, v_ref, gate_ref, o_ref, kaug_scr, vaug_scr, qaug_scr):
    p = pl.program_id(1)
    t_len = k_ref.shape[1]
    nb = t_len // BLOCK
    width = q_ref.shape[2]
    dh = width // 2
    lane = lax.broadcasted_iota(jnp.int32, (1, width), 1)

    k = k_ref[0]
    v = v_ref[0]
    row = lax.broadcasted_iota(jnp.int32, k.shape, 0)
    ln = lax.broadcasted_iota(jnp.int32, k.shape, 1)
    in_block = (row % BLOCK).astype(F32)
    block_start = (row - row % BLOCK).astype(F32)
    jrow = lax.broadcasted_iota(jnp.int32, (nb, t_len), 0)
    qblk = lax.broadcasted_iota(jnp.int32, (nb, t_len), 1) // BLOCK
    crow = lax.broadcasted_iota(jnp.int32, (AUG_ROWS, t_len), 0)
    for a in range(2):
        x0 = (1 - a) * dh
        y0 = x0 + AUG_ROWS
        extra = jnp.where(ln < x0 + SLOPE_PARTS, in_block,
                          jnp.where(ln < x0 + 2 * SLOPE_PARTS, block_start,
                                    (ln - y0 == row // BLOCK).astype(F32)))
        kaug_scr[a] = jnp.where((ln >= x0) & (ln < y0 + nb), extra.astype(BF16), k)
        vaug_scr[a] = jnp.where(ln == x0, jnp.ones_like(v), v)

        g = jnp.where(jrow < qblk, gate_ref[0, a * nb:(a + 1) * nb, :], -jnp.inf)
        rank = jnp.zeros((nb, t_len), jnp.int32)
        for i in range(nb - 1):
            gi = g[i:i + 1, :]
            rank = rank + ((gi > g) | ((gi == g) & (i < jrow))).astype(jnp.int32)
        sel_t = jnp.where((rank < TOPK) | (jrow >= qblk), 0.0, NEG)

        slope_t = jnp.zeros((AUG_ROWS, t_len), F32)
        for i in range(SLOPE_PARTS):
            slope_t = jnp.where((crow == i) | (crow == SLOPE_PARTS + i), slope_ref[i, 2 * p + a], slope_t)
        pieces = [slope_t, sel_t, jnp.zeros((width - x0 - AUG_ROWS - nb, t_len), F32)]
        if x0:
            pieces.insert(0, jnp.zeros((x0, t_len), F32))
        aug = jnp.concatenate(pieces, axis=0).T
        own = (lane >= a * dh) & (lane < (a + 1) * dh)
        qaug_scr[a] = jnp.where(own, q_ref[0].astype(F32), aug).astype(BF16)

    r = lax.broadcasted_iota(jnp.int32, (BLOCK, BLOCK), 0)
    c = lax.broadcasted_iota(jnp.int32, (BLOCK, BLOCK), 1)
    causal = r >= c

    def head(n, a):
        x0 = (1 - a) * dh
        kv_len = (n + 1) * BLOCK
        s = _dot_nt(qaug_scr[a, n * BLOCK:kv_len, :], kaug_scr[a, 0:kv_len, :])
        s_own = jnp.where(causal, s[:, n * BLOCK:], NEG)
        s = s_own if n == 0 else jnp.concatenate([s[:, :n * BLOCK], s_own], axis=1)
        m = jnp.max(s, axis=-1, keepdims=True)
        acc = _dot(jnp.exp(s - m).astype(BF16), vaug_scr[a, 0:kv_len, :])
        return acc / acc[:, x0:x0 + 1]

    for n in range(nb):
        o_ref[0, n * BLOCK:(n + 1) * BLOCK, :] = jnp.where(lane < dh, head(n, 0), head(n, 1)).astype(o_ref.dtype)


def _moba(aq, ak, av, gate_t):
    bsz, t, aw = aq.shape
    dh = aw // ATT_HEADS
    nb = t // BLOCK
    assert nb % 8 == 0 and AUG_ROWS + nb <= dh and 2 * SLOPE_PARTS <= AUG_ROWS
    slopes = jnp.exp2(-8.0 * jnp.arange(1, ATT_HEADS + 1, dtype=F32) / ATT_HEADS)
    parts = []
    for _ in range(SLOPE_PARTS):
        parts.append(slopes.astype(BF16).astype(F32))
        slopes = slopes - parts[-1]
    pair_spec = pl.BlockSpec((1, t, 2 * dh), lambda b, p: (b, 0, p))
    return pl.pallas_call(
        _moba_kernel,
        out_shape=jax.ShapeDtypeStruct((bsz, t, aw), BF16),
        grid=(bsz, ATT_HEADS // 2),
        in_specs=[pl.BlockSpec(memory_space=pltpu.SMEM),
                  pair_spec, pair_spec, pair_spec,
                  pl.BlockSpec((1, 2 * nb, t), lambda b, p: (b, p, 0))],
        out_specs=pair_spec,
        scratch_shapes=[pltpu.VMEM((2, t, 2 * dh), BF16) for _ in range(3)],
        compiler_params=pltpu.CompilerParams(
            dimension_semantics=("parallel", "parallel"), vmem_limit_bytes=VMEM_LIMIT),
        name="moba",
    )(jnp.stack(parts), aq, ak, av, gate_t)


def _ffn_kernel(x_ref, mh_ref, ma_ref, mod_ref, g2_ref, wo_ref, wg_ref, wu_ref, wd_ref, o_ref, *, ff_chunks):
    hw = mh_ref.shape[2]
    mod = mod_ref[0]
    mixed = _dot(mh_ref[0], wo_ref[0:hw]) + _dot(ma_ref[0], wo_ref[hw:])
    x1 = x_ref[0] + mod[2:3] * mixed
    xn = x1 * lax.rsqrt(jnp.mean(x1 * x1, axis=-1, keepdims=True) + EPS)
    hb = ((xn * g2_ref[...]) * (1.0 + mod[4:5]) + mod[3:4]).astype(BF16)
    fc = wg_ref.shape[1] // ff_chunks
    y = jnp.zeros_like(x1)
    for ci in range(ff_chunks):
        act = _silu(_dot(hb, wg_ref[:, ci * fc:(ci + 1) * fc])) * _dot(hb, wu_ref[:, ci * fc:(ci + 1) * fc])
        y = y + _dot(act.astype(BF16), wd_ref[ci * fc:(ci + 1) * fc])
    o_ref[0] = x1 + mod[5:6] * y


def _ffn(x, mh, ma, mod, g2, w_out, w_gate, w_up, w_down, *, tm=512, ff_chunks=2):
    bsz, t, d = x.shape
    tok = lambda w: pl.BlockSpec((1, tm, w), lambda b, i: (b, i, 0))
    const = lambda shape: pl.BlockSpec(shape, lambda b, i: (0,) * len(shape), pipeline_mode=pl.Buffered(1))
    return pl.pallas_call(
        functools.partial(_ffn_kernel, ff_chunks=ff_chunks),
        out_shape=jax.ShapeDtypeStruct((bsz, t, d), x.dtype),
        grid=(bsz, t // tm),
        in_specs=[tok(d), tok(mh.shape[2]), tok(ma.shape[2]),
                  pl.BlockSpec((1, 6, d), lambda b, i: (b, 0, 0)),
                  const((1, d)), const(w_out.shape), const(w_gate.shape), const(w_up.shape), const(w_down.shape)],
        out_specs=tok(d),
        compiler_params=pltpu.CompilerParams(
            dimension_semantics=("parallel", "parallel"), vmem_limit_bytes=VMEM_LIMIT),
        name="ffn",
    )(x, mh, ma, mod, g2.reshape(1, d), w_out.astype(BF16), w_gate.astype(BF16), w_up.astype(BF16),
      w_down.astype(BF16))


def kernel(x, c, w_ada, b_ada, norm1_g, w_in, lb_logits, hg_norm_g, q_norm_g, k_norm_g,
           w_out, norm2_g, w_gate, w_up, w_down):
    bsz, _, d = x.shape
    for l in range(w_ada.shape[0]):
        mod = _adaln(c, w_ada[l], b_ada[l]).reshape(bsz, 6, d)
        hq, kk, lf, hv, hg, aq, ak, av, gate = _inproj(
            x, mod, norm1_g[l], w_in[l], lb_logits, q_norm_g[l], k_norm_g[l], layer=l)
        o_hg = _hgrn(hq, kk, lf, hv, hg, hg_norm_g[l])
        o_att = _moba(aq, ak, av, gate)
        x = _ffn(x, o_hg, o_att, mod, norm2_g[l], w_out[l], w_gate[l], w_up[l], w_down[l])
    return x
```

```python
import functools

import jax
import jax.numpy as jnp
from jax import lax
from jax.experimental import pallas as pl
from jax.experimental.pallas import tpu as pltpu

F32 = jnp.float32
BF16 = jnp.bfloat16

HG_HEADS = 4
ATT_HEADS = 8
BLOCK = 256
TOPK = 3
HG_CHUNK = 256
EPS = 1e-6
NEG = -(2.0 ** 100)

LOG2E = 1.4426950408889634
MXU_TILE = 256

VMEM_LIMIT = 56 * 1024 * 1024


def _silu(t):
    return t * jax.nn.sigmoid(t)


def _dot(a, b):
    return jnp.dot(a, b, preferred_element_type=F32)


def _dot_nt(a, b):
    return lax.dot_general(a, b, (((1,), (1,)), ((), ())), preferred_element_type=F32)


def _dot_tn(a, b):
    return lax.dot_general(a, b, (((0,), (0,)), ((), ())), preferred_element_type=F32)


def _split2(t):
    hi = t.astype(BF16)
    lo = (t - hi.astype(F32)).astype(BF16)
    return hi, lo


def _dot3(a, b, dot=_dot):
    a_hi, a_lo = _split2(a)
    b_hi, b_lo = _split2(b)
    return (dot(a_hi, b_lo) + dot(a_lo, b_hi)) + dot(a_hi, b_hi)


def _adaln_kernel(c_ref, w_ref, b_ref, o_ref):
    o_ref[...] = _dot3(_silu(c_ref[...]), w_ref[...]) + b_ref[...]


def _adaln(c, w, b):
    bsz, d = c.shape
    n = w.shape[1]
    tn = 1024
    return pl.pallas_call(
        _adaln_kernel,
        out_shape=jax.ShapeDtypeStruct((bsz, n), F32),
        grid=(n // tn,),
        in_specs=[pl.BlockSpec((bsz, d), lambda j: (0, 0)),
                  pl.BlockSpec((d, tn), lambda j: (0, j)),
                  pl.BlockSpec((1, tn), lambda j: (0, j))],
        out_specs=pl.BlockSpec((bsz, tn), lambda j: (0, j)),
        name="adaln",
    )(c, w, b.reshape(1, n))


def _inproj_kernel(x_ref, mod_ref, g1_ref, w_ref, lbl_ref, qg_ref, kg_ref, pool_ref,
                   hq_ref, kk_ref, lf_ref, hv_ref, hg_ref, aq_ref, ak_ref, av_ref, gate_ref,
                   kmean_scr, *, tm, hw, aw, layer):
    i = pl.program_id(1)
    dh = aw // ATT_HEADS
    nb = kmean_scr.shape[0]

    x = x_ref[0]
    xn = x * lax.rsqrt(jnp.mean(x * x, axis=-1, keepdims=True) + EPS)
    mod = mod_ref[0]
    h = (xn * g1_ref[...]) * (1.0 + mod[1:2]) + mod[0:1]
    hb = h.astype(BF16)

    def proj(lo, width):
        return _dot(hb, w_ref[:, lo:lo + width])

    def head_norm(t, gain):
        t2 = (t * t).astype(BF16)
        pw = pool_ref.shape[0]
        ms = jnp.concatenate([_dot(t2[:, c:c + pw], pool_ref[...]) for c in range(0, aw, pw)], axis=1)
        return t * lax.rsqrt(ms + EPS) * gain

    kn = head_norm(proj(4 * hw + aw, aw), kg_ref[...])
    ak_ref[0] = kn.astype(BF16)

    @pl.when(i == 0)
    def _():
        kmean_scr[...] = jnp.zeros_like(kmean_scr)

    per_tile = tm // BLOCK
    for s in range(per_tile):
        kmean_scr[pl.ds(i * per_tile + s, 1), :] = jnp.mean(
            kn[s * BLOCK:(s + 1) * BLOCK], axis=0, keepdims=True)

    qn = head_norm(proj(4 * hw, aw), qg_ref[...])
    aq_ref[0] = (qn * (dh ** -0.5 * LOG2E)).astype(BF16)
    kmean = kmean_scr[...]
    rows = lax.broadcasted_iota(jnp.int32, (ATT_HEADS * nb, aw), 0)
    cols = lax.broadcasted_iota(jnp.int32, (ATT_HEADS * nb, aw), 1)
    sel = jnp.where(rows // nb == cols // dh, jnp.concatenate([kmean] * ATT_HEADS, axis=0), 0.0)
    gate_ref[0] = _dot3(sel, qn, dot=_dot_nt)
    av_ref[0] = proj(4 * hw + 2 * aw, aw).astype(BF16)

    hq_ref[0] = _silu(proj(0, hw)).astype(BF16)
    lbl = lbl_ref[...]
    e = jnp.exp(lbl - jnp.max(lbl, axis=0, keepdims=True))
    lb = jnp.sum(e[0:layer + 1], axis=0, keepdims=True) / jnp.sum(e, axis=0, keepdims=True)
    f = lb + (1.0 - lb) * jax.nn.sigmoid(proj(hw, hw))
    lf_ref[0] = jnp.log2(f)
    kk_ref[0] = (1.0 - f).astype(BF16)
    hv_ref[0] = proj(2 * hw, hw).astype(BF16)
    hg_ref[0] = _silu(proj(3 * hw, hw)).astype(BF16)


def _inproj(x, mod, g1, w_in, lb_logits, qg, kg, *, layer, tm=512):
    bsz, t, d = x.shape
    hw = lb_logits.shape[1]
    aw = (w_in.shape[1] - 4 * hw) // 3
    dh = aw // ATT_HEADS
    nb = t // BLOCK
    lanes = jnp.arange(MXU_TILE)
    pool = jnp.where(lanes[:, None] // dh == lanes[None, :] // dh, 1.0 / dh, 0.0).astype(BF16)
    tok = lambda w: pl.BlockSpec((1, tm, w), lambda b, i: (b, i, 0))
    const = lambda shape: pl.BlockSpec(shape, lambda b, i: (0,) * len(shape), pipeline_mode=pl.Buffered(1))
    bf = lambda w: jax.ShapeDtypeStruct((bsz, t, w), BF16)
    return pl.pallas_call(
        functools.partial(_inproj_kernel, tm=tm, hw=hw, aw=aw, layer=layer),
        out_shape=[bf(hw), bf(hw), jax.ShapeDtypeStruct((bsz, t, hw), F32), bf(hw), bf(hw),
                   bf(aw), bf(aw), bf(aw),
                   jax.ShapeDtypeStruct((bsz, ATT_HEADS * nb, t), F32)],
        grid=(bsz, t // tm),
        in_specs=[tok(d),
                  pl.BlockSpec((1, 6, d), lambda b, i: (b, 0, 0)),
                  const((1, d)), const(w_in.shape), const(lb_logits.shape),
                  const((1, aw)), const((1, aw)), const((MXU_TILE, MXU_TILE))],
        out_specs=[tok(hw), tok(hw), tok(hw), tok(hw), tok(hw), tok(aw), tok(aw), tok(aw),
                   pl.BlockSpec((1, ATT_HEADS * nb, tm), lambda b, i: (b, 0, i))],
        scratch_shapes=[pltpu.VMEM((nb, aw), F32)],
        compiler_params=pltpu.CompilerParams(
            dimension_semantics=("parallel", "arbitrary"), vmem_limit_bytes=VMEM_LIMIT),
        name="inproj",
    )(x, mod, g1.reshape(1, d), w_in.astype(BF16), lb_logits,
      jnp.tile(qg, ATT_HEADS).reshape(1, aw), jnp.tile(kg, ATT_HEADS).reshape(1, aw), pool)


HG_LEVELS = (128, 64, 32, 16, 8)
HG_DIAG = 8
HG_UNROLL = 1
BF16_ROWS = 16


def _group_rows(b, first, step, rows):
    n = b.shape[0] // rows
    return jnp.concatenate(
        [jnp.broadcast_to(b[first + g * step:first + g * step + 1], (rows, b.shape[1])) for g in range(n)], axis=0)


def _blend_rows(q, kk, m):
    n = q.shape[0] // m
    return jnp.concatenate([(q if g % 2 else kk)[g * m:(g + 1) * m] for g in range(n)], axis=0)


def _hgrn_kernel(q_ref, k_ref, lf_ref, v_ref, g_ref, gain_ref, o_ref, st_ref):
    c_len = HG_CHUNK
    t_len, hw = q_ref.shape[1], q_ref.shape[2]
    d = hw // HG_HEADS
    r = lax.broadcasted_iota(jnp.int32, (c_len, c_len), 0)
    c = lax.broadcasted_iota(jnp.int32, (c_len, c_len), 1)
    tri = (r >= c).astype(BF16)
    level_masks = [(r // (2 * m) == c // (2 * m)) & ((r // m) % 2 == 1) & ((c // m) % 2 == 0) for m in HG_LEVELS]
    diag_mask = (r // HG_DIAG == c // HG_DIAG) & (r >= c)
    gain = gain_ref[...]

    st_ref[...] = jnp.zeros_like(st_ref)

    def body(ci, carry):
        sl = pl.ds(pl.multiple_of(ci * c_len, c_len), c_len)
        q = q_ref[0, sl, :]
        kk = k_ref[0, sl, :]
        lf = lf_ref[0, sl, :]
        v = v_ref[0, sl, :]

        l1 = lf.astype(BF16)
        r1 = lf - l1.astype(F32)
        l2 = r1.astype(BF16)
        l3 = (r1 - l2.astype(F32)).astype(BF16)
        b = (_dot(tri, l3) + _dot(tri, l2)) + _dot(tri, l1)
        b_last = b[c_len - 1:c_len]

        q_in = q * jnp.exp2(b).astype(BF16)
        k_st = kk * jnp.exp2(b_last - b).astype(BF16)
        decay = jnp.exp2(b_last)
        pairs = []
        for m in HG_LEVELS:
            w = jnp.exp2(-jnp.abs(b - _group_rows(b, m - 1, 2 * m, 2 * m)))
            if m % BF16_ROWS == 0:
                x = _blend_rows(q, kk, m) * w.astype(BF16)
            else:
                x = (_blend_rows(q.astype(F32), kk.astype(F32), m) * w).astype(BF16)
            pairs.append((x, x))
        dd = b - _group_rows(b, 0, HG_DIAG, HG_DIAG)
        pairs.append((q * jnp.exp2(dd).astype(BF16), kk * jnp.exp2(-dd).astype(BF16)))

        outs = []
        for h in range(HG_HEADS):
            hs = slice(h * d, (h + 1) * d)
            st = st_ref[h]
            o = _dot_nt(q_in[:, hs], st.astype(BF16))
            st_ref[h] = decay[:, hs] * st + _dot_tn(v[:, hs], k_st[:, hs])
            attn = jnp.zeros((c_len, c_len), F32)
            for (qw, kw), mask in zip(pairs, level_masks + [diag_mask]):
                attn = jnp.where(mask, _dot_nt(qw[:, hs], kw[:, hs]), attn)
            o = o + _dot(attn.astype(BF16), v[:, hs])
            outs.append(o * lax.rsqrt(jnp.mean(o * o, axis=-1, keepdims=True) + EPS) * gain)
        o_ref[0, sl, :] = (jnp.concatenate(outs, axis=1) * g_ref[0, sl, :].astype(F32)).astype(o_ref.dtype)
        return carry

    lax.fori_loop(0, t_len // c_len, body, 0, unroll=HG_UNROLL)


def _hgrn(hq, kk, lf, hv, hg, gain):
    bsz, t, hw = hq.shape
    d = hw // HG_HEADS
    blk = pl.BlockSpec((1, t, hw), lambda b: (b, 0, 0))
    return pl.pallas_call(
        _hgrn_kernel,
        out_shape=jax.ShapeDtypeStruct((bsz, t, hw), BF16),
        grid=(bsz,),
        in_specs=[blk, blk, blk, blk, blk, pl.BlockSpec((1, d), lambda b: (0, 0))],
        out_specs=blk,
        scratch_shapes=[pltpu.VMEM((HG_HEADS, d, d), F32)],
        compiler_params=pltpu.CompilerParams(dimension_semantics=("parallel",), vmem_limit_bytes=VMEM_LIMIT),
        name="hgrn",
    )(hq, kk, lf, hv, hg, gain.reshape(1, d))


SLOPE_PARTS = 3
AUG_ROWS = 8


def _moba_kernel(slope_ref, q_ref, k_ref, v_ref, gate_ref, o_ref, kaug_scr, vaug_scr, qaug_scr):
    p = pl.program_id(1)
    t_len = k_ref.shape[1]
    nb = t_len // BLOCK
    width = q_ref.shape[2]
    dh = width // 2
    lane = lax.broadcasted_iota(jnp.int32, (1, width), 1)

    k = k_ref[0]
    v = v_ref[0]
    row = lax.broadcasted_iota(jnp.int32, k.shape, 0)
    ln = lax.broadcasted_iota(jnp.int32, k.shape, 1)
    in_block = (row % BLOCK).astype(F32)
    block_start = (row - row % BLOCK).astype(F32)
    jrow = lax.broadcasted_iota(jnp.int32, (nb, t_len), 0)
    qblk = lax.broadcasted_iota(jnp.int32, (nb, t_len), 1) // BLOCK
    crow = lax.broadcasted_iota(jnp.int32, (AUG_ROWS, t_len), 0)
    for a in range(2):
        x0 = (1 - a) * dh
        y0 = x0 + AUG_ROWS
        extra = jnp.where(ln < x0 + SLOPE_PARTS, in_block,
                          jnp.where(ln < x0 + 2 * SLOPE_PARTS, block_start,
                                    (ln - y0 == row // BLOCK).astype(F32)))
        kaug_scr[a] = jnp.where((ln >= x0) & (ln < y0 + nb), extra.astype(BF16), k)
        vaug_scr[a] = jnp.where(ln == x0, jnp.ones_like(v), v)

        g = jnp.where(jrow < qblk, gate_ref[0, a * nb:(a + 1) * nb, :], -jnp.inf)
        rank = jnp.zeros((nb, t_len), jnp.int32)
        for i in range(nb - 1):
            gi = g[i:i + 1, :]
            rank = rank + ((gi > g) | ((gi == g) & (i < jrow))).astype(jnp.int32)
        sel_t = jnp.where((rank < TOPK) | (jrow >= qblk), 0.0, NEG)

        slope_t = jnp.zeros((AUG_ROWS, t_len), F32)
        for i in range(SLOPE_PARTS):
            slope_t = jnp.where((crow == i) | (crow == SLOPE_PARTS + i), slope_ref[i, 2 * p + a], slope_t)
        pieces = [slope_t, sel_t, jnp.zeros((width - x0 - AUG_ROWS - nb, t_len), F32)]
        if x0:
            pieces.insert(0, jnp.zeros((x0, t_len), F32))
        aug = jnp.concatenate(pieces, axis=0).T
        own = (lane >= a * dh) & (lane < (a + 1) * dh)
        qaug_scr[a] = jnp.where(own, q_ref[0].astype(F32), aug).astype(BF16)

    r = lax.broadcasted_iota(jnp.int32, (BLOCK, BLOCK), 0)
    c = lax.broadcasted_iota(jnp.int32, (BLOCK, BLOCK), 1)
    causal = r >= c

    def head(n, a):
        x0 = (1 - a) * dh
        kv_len = (n + 1) * BLOCK
        s = _dot_nt(qaug_scr[a, n * BLOCK:kv_len, :], kaug_scr[a, 0:kv_len, :])
        s_own = jnp.where(causal, s[:, n * BLOCK:], NEG)
        s = s_own if n == 0 else jnp.concatenate([s[:, :n * BLOCK], s_own], axis=1)
        m = jnp.max(s, axis=-1, keepdims=True)
        acc = _dot(jnp.exp2(s - m).astype(BF16), vaug_scr[a, 0:kv_len, :])
        return acc / acc[:, x0:x0 + 1]

    for n in range(nb):
        o_ref[0, n * BLOCK:(n + 1) * BLOCK, :] = jnp.where(lane < dh, head(n, 0), head(n, 1)).astype(o_ref.dtype)


def _moba(aq, ak, av, gate_t):
    bsz, t, aw = aq.shape
    dh = aw // ATT_HEADS
    nb = t // BLOCK
    assert nb % 8 == 0 and AUG_ROWS + nb <= dh and 2 * SLOPE_PARTS <= AUG_ROWS
    slopes = LOG2E * jnp.exp2(-8.0 * jnp.arange(1, ATT_HEADS + 1, dtype=F32) / ATT_HEADS)
    parts = []
    for _ in range(SLOPE_PARTS):
        parts.append(slopes.astype(BF16).astype(F32))
        slopes = slopes - parts[-1]
    pair_spec = pl.BlockSpec((1, t, 2 * dh), lambda b, p: (b, 0, p))
    return pl.pallas_call(
        _moba_kernel,
        out_shape=jax.ShapeDtypeStruct((bsz, t, aw), BF16),
        grid=(bsz, ATT_HEADS // 2),
        in_specs=[pl.BlockSpec(memory_space=pltpu.SMEM),
                  pair_spec, pair_spec, pair_spec,
                  pl.BlockSpec((1, 2 * nb, t), lambda b, p: (b, p, 0))],
        out_specs=pair_spec,
        scratch_shapes=[pltpu.VMEM((2, t, 2 * dh), BF16) for _ in range(3)],
        compiler_params=pltpu.CompilerParams(
            dimension_semantics=("parallel", "parallel"), vmem_limit_bytes=VMEM_LIMIT),
        name="moba",
    )(jnp.stack(parts), aq, ak, av, gate_t)


def _ffn_kernel(x_ref, mh_ref, ma_ref, mod_ref, g2_ref, wo_ref, wg_ref, wu_ref, wd_ref, o_ref, *, ff_chunks):
    hw = mh_ref.shape[2]
    mod = mod_ref[0]
    mixed = _dot(mh_ref[0], wo_ref[0:hw]) + _dot(ma_ref[0], wo_ref[hw:])
    x1 = x_ref[0] + mod[2:3] * mixed
    xn = x1 * lax.rsqrt(jnp.mean(x1 * x1, axis=-1, keepdims=True) + EPS)
    hb = ((xn * g2_ref[...]) * (1.0 + mod[4:5]) + mod[3:4]).astype(BF16)
    tiles = wg_ref.shape[1] // MXU_TILE
    bounds = [-(-tiles * ci // ff_chunks) * MXU_TILE for ci in range(ff_chunks)] + [wg_ref.shape[1]]
    y = jnp.zeros_like(x1)
    for lo, hi in zip(bounds[:-1], bounds[1:]):
        act = _silu(_dot(hb, wg_ref[:, lo:hi])) * _dot(hb, wu_ref[:, lo:hi])
        y = y + _dot(act.astype(BF16), wd_ref[lo:hi])
    o_ref[0] = x1 + mod[5:6] * y


def _ffn(x, mh, ma, mod, g2, w_out, w_gate, w_up, w_down, *, tm=512, ff_chunks=2):
    bsz, t, d = x.shape
    tok = lambda w: pl.BlockSpec((1, tm, w), lambda b, i: (b, i, 0))
    const = lambda shape: pl.BlockSpec(shape, lambda b, i: (0,) * len(shape), pipeline_mode=pl.Buffered(1))
    return pl.pallas_call(
        functools.partial(_ffn_kernel, ff_chunks=ff_chunks),
        out_shape=jax.ShapeDtypeStruct((bsz, t, d), x.dtype),
        grid=(bsz, t // tm),
        in_specs=[tok(d), tok(mh.shape[2]), tok(ma.shape[2]),
                  pl.BlockSpec((1, 6, d), lambda b, i: (b, 0, 0)),
                  const((1, d)), const(w_out.shape), const(w_gate.shape), const(w_up.shape), const(w_down.shape)],
        out_specs=tok(d),
        compiler_params=pltpu.CompilerParams(
            dimension_semantics=("parallel", "parallel"), vmem_limit_bytes=VMEM_LIMIT),
        name="ffn",
    )(x, mh, ma, mod, g2.reshape(1, d), w_out.astype(BF16), w_gate.astype(BF16), w_up.astype(BF16),
      w_down.astype(BF16))


def kernel(x, c, w_ada, b_ada, norm1_g, w_in, lb_logits, hg_norm_g, q_norm_g, k_norm_g,
           w_out, norm2_g, w_gate, w_up, w_down):
    bsz, _, d = x.shape
    for l in range(w_ada.shape[0]):
        mod = _adaln(c, w_ada[l], b_ada[l]).reshape(bsz, 6, d)
        hq, kk, lf, hv, hg, aq, ak, av, gate = _inproj(
            x, mod, norm1_g[l], w_in[l], lb_logits, q_norm_g[l], k_norm_g[l], layer=l)
        o_hg = _hgrn(hq, kk, lf, hv, hg, hg_norm_g[l])
        o_att = _moba(aq, ak, av, gate)
        x = _ffn(x, o_hg, o_att, mod, norm2_g[l], w_out[l], w_gate[l], w_up[l], w_down[l])
    return x
```

```python
import functools

import jax
import jax.numpy as jnp
from jax import lax
from jax.experimental import pallas as pl
from jax.experimental.pallas import tpu as pltpu

F32 = jnp.float32
BF16 = jnp.bfloat16

HG_HEADS = 4
ATT_HEADS = 8
BLOCK = 256
TOPK = 3
HG_CHUNK = 256
EPS = 1e-6
NEG = -(2.0 ** 100)

LOG2E = 1.4426950408889634
MXU_TILE = 256

VMEM_LIMIT = 56 * 1024 * 1024


def _silu(t):
    return t * jax.nn.sigmoid(t)


def _dot(a, b):
    return jnp.dot(a, b, preferred_element_type=F32)


def _dot_nt(a, b):
    return lax.dot_general(a, b, (((1,), (1,)), ((), ())), preferred_element_type=F32)


def _dot_tn(a, b):
    return lax.dot_general(a, b, (((0,), (0,)), ((), ())), preferred_element_type=F32)


def _split2(t):
    hi = t.astype(BF16)
    lo = (t - hi.astype(F32)).astype(BF16)
    return hi, lo


def _dot3(a, b, dot=_dot):
    a_hi, a_lo = _split2(a)
    b_hi, b_lo = _split2(b)
    return (dot(a_hi, b_lo) + dot(a_lo, b_hi)) + dot(a_hi, b_hi)


def _adaln_kernel(c_ref, w_ref, b_ref, o_ref):
    o_ref[...] = _dot3(_silu(c_ref[...]), w_ref[...]) + b_ref[...]


def _adaln(c, w, b):
    bsz, d = c.shape
    n = w.shape[1]
    tn = 1024
    return pl.pallas_call(
        _adaln_kernel,
        out_shape=jax.ShapeDtypeStruct((bsz, n), F32),
        grid=(n // tn,),
        in_specs=[pl.BlockSpec((bsz, d), lambda j: (0, 0)),
                  pl.BlockSpec((d, tn), lambda j: (0, j)),
                  pl.BlockSpec((1, tn), lambda j: (0, j))],
        out_specs=pl.BlockSpec((bsz, tn), lambda j: (0, j)),
        name="adaln",
    )(c, w, b.reshape(1, n))


def _inproj_kernel(x_ref, mod_ref, g1_ref, w_ref, wvt_ref, lbl_ref, qg_ref, kg_ref, pool_ref,
                   hq_ref, kk_ref, lf_ref, hv_ref, hg_ref, aq_ref, ak_ref, avt_ref, gate_ref,
                   kmean_scr, *, tm, hw, aw, layer):
    i = pl.program_id(1)
    dh = aw // ATT_HEADS
    nb = kmean_scr.shape[0]

    x = x_ref[0]
    xn = x * lax.rsqrt(jnp.mean(x * x, axis=-1, keepdims=True) + EPS)
    mod = mod_ref[0]
    h = (xn * g1_ref[...]) * (1.0 + mod[1:2]) + mod[0:1]
    hb = h.astype(BF16)

    def proj(lo, width):
        return _dot(hb, w_ref[:, lo:lo + width])

    def head_norm(t, gain):
        t2 = (t * t).astype(BF16)
        pw = pool_ref.shape[0]
        ms = jnp.concatenate([_dot(t2[:, c:c + pw], pool_ref[...]) for c in range(0, aw, pw)], axis=1)
        return t * lax.rsqrt(ms + EPS) * gain

    kn = head_norm(proj(4 * hw + aw, aw), kg_ref[...])
    ak_ref[0] = kn.astype(BF16)

    @pl.when(i == 0)
    def _():
        kmean_scr[...] = jnp.zeros_like(kmean_scr)

    per_tile = tm // BLOCK
    for s in range(per_tile):
        kmean_scr[pl.ds(i * per_tile + s, 1), :] = jnp.mean(
            kn[s * BLOCK:(s + 1) * BLOCK], axis=0, keepdims=True)

    qn = head_norm(proj(4 * hw, aw), qg_ref[...])
    aq_ref[0] = (qn * (dh ** -0.5 * LOG2E)).astype(BF16)
    kmean = kmean_scr[...]
    rows = lax.broadcasted_iota(jnp.int32, (ATT_HEADS * nb, aw), 0)
    cols = lax.broadcasted_iota(jnp.int32, (ATT_HEADS * nb, aw), 1)
    sel = jnp.where(rows // nb == cols // dh, jnp.concatenate([kmean] * ATT_HEADS, axis=0), 0.0)
    gate_ref[0] = _dot3(sel, qn, dot=_dot_nt)
    avt_ref[0] = _dot_nt(wvt_ref[...], hb).astype(BF16)

    hq_ref[0] = _silu(proj(0, hw)).astype(BF16)
    lbl = lbl_ref[...]
    e = jnp.exp(lbl - jnp.max(lbl, axis=0, keepdims=True))
    lb = jnp.sum(e[0:layer + 1], axis=0, keepdims=True) / jnp.sum(e, axis=0, keepdims=True)
    f = lb + (1.0 - lb) * jax.nn.sigmoid(proj(hw, hw))
    lf_ref[0] = jnp.log2(f)
    kk_ref[0] = (1.0 - f).astype(BF16)
    hv_ref[0] = proj(2 * hw, hw).astype(BF16)
    hg_ref[0] = _silu(proj(3 * hw, hw)).astype(BF16)


def _inproj(x, mod, g1, w_in, lb_logits, qg, kg, *, layer, tm=512):
    bsz, t, d = x.shape
    hw = lb_logits.shape[1]
    aw = (w_in.shape[1] - 4 * hw) // 3
    dh = aw // ATT_HEADS
    nb = t // BLOCK
    lanes = jnp.arange(MXU_TILE)
    pool = jnp.where(lanes[:, None] // dh == lanes[None, :] // dh, 1.0 / dh, 0.0).astype(BF16)
    tok = lambda w: pl.BlockSpec((1, tm, w), lambda b, i: (b, i, 0))
    const = lambda shape: pl.BlockSpec(shape, lambda b, i: (0,) * len(shape), pipeline_mode=pl.Buffered(1))
    bf = lambda w: jax.ShapeDtypeStruct((bsz, t, w), BF16)
    tok_t = lambda rows: pl.BlockSpec((1, rows, tm), lambda b, i: (b, 0, i))
    w_main = w_in[:, :4 * hw + 2 * aw].astype(BF16)
    w_v_t = w_in[:, 4 * hw + 2 * aw:].T.astype(BF16)
    return pl.pallas_call(
        functools.partial(_inproj_kernel, tm=tm, hw=hw, aw=aw, layer=layer),
        out_shape=[bf(hw), bf(hw), jax.ShapeDtypeStruct((bsz, t, hw), F32), bf(hw), bf(hw),
                   bf(aw), bf(aw), jax.ShapeDtypeStruct((bsz, aw, t), BF16),
                   jax.ShapeDtypeStruct((bsz, ATT_HEADS * nb, t), F32)],
        grid=(bsz, t // tm),
        in_specs=[tok(d),
                  pl.BlockSpec((1, 6, d), lambda b, i: (b, 0, 0)),
                  const((1, d)), const(w_main.shape), const(w_v_t.shape), const(lb_logits.shape),
                  const((1, aw)), const((1, aw)), const((MXU_TILE, MXU_TILE))],
        out_specs=[tok(hw), tok(hw), tok(hw), tok(hw), tok(hw), tok(aw), tok(aw), tok_t(aw),
                   tok_t(ATT_HEADS * nb)],
        scratch_shapes=[pltpu.VMEM((nb, aw), F32)],
        compiler_params=pltpu.CompilerParams(
            dimension_semantics=("parallel", "arbitrary"), vmem_limit_bytes=VMEM_LIMIT),
        name="inproj",
    )(x, mod, g1.reshape(1, d), w_main, w_v_t, lb_logits,
      jnp.tile(qg, ATT_HEADS).reshape(1, aw), jnp.tile(kg, ATT_HEADS).reshape(1, aw), pool)


HG_LEVELS = (128, 64, 32, 16, 8)
HG_DIAG = 8
HG_UNROLL = 1
BF16_ROWS = 16


def _group_rows(b, first, step, rows):
    n = b.shape[0] // rows
    return jnp.concatenate(
        [jnp.broadcast_to(b[first + g * step:first + g * step + 1], (rows, b.shape[1])) for g in range(n)], axis=0)


def _blend_rows(q, kk, m):
    n = q.shape[0] // m
    return jnp.concatenate([(q if g % 2 else kk)[g * m:(g + 1) * m] for g in range(n)], axis=0)


def _hgrn_kernel(q_ref, k_ref, lf_ref, v_ref, g_ref, gain_ref, o_ref, st_ref):
    c_len = HG_CHUNK
    t_len, hw = q_ref.shape[1], q_ref.shape[2]
    d = hw // HG_HEADS
    r = lax.broadcasted_iota(jnp.int32, (c_len, c_len), 0)
    c = lax.broadcasted_iota(jnp.int32, (c_len, c_len), 1)
    tri = (r >= c).astype(BF16)
    level_masks = [(r // (2 * m) == c // (2 * m)) & ((r // m) % 2 == 1) & ((c // m) % 2 == 0) for m in HG_LEVELS]
    diag_mask = (r // HG_DIAG == c // HG_DIAG) & (r >= c)
    gain = gain_ref[...]

    st_ref[...] = jnp.zeros_like(st_ref)

    def body(ci, carry):
        sl = pl.ds(pl.multiple_of(ci * c_len, c_len), c_len)
        q = q_ref[0, sl, :]
        kk = k_ref[0, sl, :]
        lf = lf_ref[0, sl, :]
        v = v_ref[0, sl, :]

        l1 = lf.astype(BF16)
        r1 = lf - l1.astype(F32)
        l2 = r1.astype(BF16)
        l3 = (r1 - l2.astype(F32)).astype(BF16)
        b = (_dot(tri, l3) + _dot(tri, l2)) + _dot(tri, l1)
        b_last = b[c_len - 1:c_len]

        q_in = q * jnp.exp2(b).astype(BF16)
        k_st = kk * jnp.exp2(b_last - b).astype(BF16)
        decay = jnp.exp2(b_last)
        pairs = []
        for m in HG_LEVELS:
            w = jnp.exp2(-jnp.abs(b - _group_rows(b, m - 1, 2 * m, 2 * m)))
            if m % BF16_ROWS == 0:
                x = _blend_rows(q, kk, m) * w.astype(BF16)
            else:
                x = (_blend_rows(q.astype(F32), kk.astype(F32), m) * w).astype(BF16)
            pairs.append((x, x))
        dd = b - _group_rows(b, 0, HG_DIAG, HG_DIAG)
        pairs.append((q * jnp.exp2(dd).astype(BF16), kk * jnp.exp2(-dd).astype(BF16)))

        outs = []
        for h in range(HG_HEADS):
            hs = slice(h * d, (h + 1) * d)
            st = st_ref[h]
            o = _dot_nt(q_in[:, hs], st.astype(BF16))
            st_ref[h] = decay[:, hs] * st + _dot_tn(v[:, hs], k_st[:, hs])
            attn = jnp.zeros((c_len, c_len), F32)
            for (qw, kw), mask in zip(pairs, level_masks + [diag_mask]):
                attn = jnp.where(mask, _dot_nt(qw[:, hs], kw[:, hs]), attn)
            o = o + _dot(attn.astype(BF16), v[:, hs])
            outs.append(o * lax.rsqrt(jnp.mean(o * o, axis=-1, keepdims=True) + EPS) * gain)
        o_ref[0, sl, :] = (jnp.concatenate(outs, axis=1) * g_ref[0, sl, :].astype(F32)).astype(o_ref.dtype)
        return carry

    lax.fori_loop(0, t_len // c_len, body, 0, unroll=HG_UNROLL)


def _hgrn(hq, kk, lf, hv, hg, gain):
    bsz, t, hw = hq.shape
    d = hw // HG_HEADS
    blk = pl.BlockSpec((1, t, hw), lambda b: (b, 0, 0))
    return pl.pallas_call(
        _hgrn_kernel,
        out_shape=jax.ShapeDtypeStruct((bsz, t, hw), BF16),
        grid=(bsz,),
        in_specs=[blk, blk, blk, blk, blk, pl.BlockSpec((1, d), lambda b: (0, 0))],
        out_specs=blk,
        scratch_shapes=[pltpu.VMEM((HG_HEADS, d, d), F32)],
        compiler_params=pltpu.CompilerParams(dimension_semantics=("parallel",), vmem_limit_bytes=VMEM_LIMIT),
        name="hgrn",
    )(hq, kk, lf, hv, hg, gain.reshape(1, d))


SLOPE_PARTS = 3
AUG_ROWS = 8
QUERY_TILE = 2 * MXU_TILE


def _moba_kernel(slope_ref, q_ref, k_ref, vt_ref, gate_ref, o_ref, kaug_scr, qaug_scr, vaug_scr):
    p = pl.program_id(1)
    t_len, width = k_ref.shape[1], k_ref.shape[2]
    nb = t_len // BLOCK
    dh = width // 2

    k = k_ref[0]
    vt = vt_ref[0]
    qt = q_ref[0].astype(F32).T
    row = lax.broadcasted_iota(jnp.int32, k.shape, 0)
    ln = lax.broadcasted_iota(jnp.int32, k.shape, 1)
    in_block = (row % BLOCK).astype(F32)
    block_start = (row - row % BLOCK).astype(F32)
    feat = lax.broadcasted_iota(jnp.int32, (width, t_len), 0)
    jrow = lax.broadcasted_iota(jnp.int32, (nb, t_len), 0)
    qblk = lax.broadcasted_iota(jnp.int32, (nb, t_len), 1) // BLOCK
    crow = lax.broadcasted_iota(jnp.int32, (AUG_ROWS, t_len), 0)
    for a in range(2):
        x0 = (1 - a) * dh
        y0 = x0 + AUG_ROWS
        extra = jnp.where(ln < x0 + SLOPE_PARTS, in_block,
                          jnp.where(ln < x0 + 2 * SLOPE_PARTS, block_start,
                                    (ln - y0 == row // BLOCK).astype(F32)))
        kaug_scr[a] = jnp.where((ln >= x0) & (ln < y0 + nb), extra.astype(BF16), k)
        vaug_scr[a] = jnp.concatenate([vt[a * dh:(a + 1) * dh], jnp.ones((BF16_ROWS, t_len), BF16)], axis=0)

        g = jnp.where(jrow < qblk, gate_ref[0, a * nb:(a + 1) * nb, :], -jnp.inf)
        rank = jnp.zeros((nb, t_len), jnp.int32)
        for i in range(nb - 1):
            gi = g[i:i + 1, :]
            rank = rank + ((gi > g) | ((gi == g) & (i < jrow))).astype(jnp.int32)
        sel_t = jnp.where((rank < TOPK) | (jrow >= qblk), 0.0, NEG)

        slope_t = jnp.zeros((AUG_ROWS, t_len), F32)
        for i in range(SLOPE_PARTS):
            slope_t = jnp.where((crow == i) | (crow == SLOPE_PARTS + i), slope_ref[i, 2 * p + a], slope_t)
        pieces = [slope_t, sel_t, jnp.zeros((width - x0 - AUG_ROWS - nb, t_len), F32)]
        if x0:
            pieces.insert(0, jnp.zeros((x0, t_len), F32))
        own = (feat >= a * dh) & (feat < (a + 1) * dh)
        qaug_scr[a] = jnp.where(own, qt, jnp.concatenate(pieces, axis=0)).astype(BF16)

    key = lax.broadcasted_iota(jnp.int32, (QUERY_TILE, QUERY_TILE), 0)
    qry = lax.broadcasted_iota(jnp.int32, (QUERY_TILE, QUERY_TILE), 1)
    causal = key <= qry

    def head(lo, a):
        hi = lo + QUERY_TILE
        s = _dot(kaug_scr[a, 0:hi, :], qaug_scr[a, :, lo:hi])
        s_own = jnp.where(causal, s[lo:], NEG)
        s = s_own if lo == 0 else jnp.concatenate([s[:lo], s_own], axis=0)
        m = jnp.max(s, axis=0, keepdims=True)
        acc = _dot(vaug_scr[a, :, 0:hi], jnp.exp2(s - m).astype(BF16))
        return acc[0:dh] / acc[dh:dh + 1]

    for lo in range(0, t_len, QUERY_TILE):
        out_t = jnp.concatenate([head(lo, 0), head(lo, 1)], axis=0)
        o_ref[0, lo:lo + QUERY_TILE, :] = out_t.T.astype(o_ref.dtype)


def _moba(aq, ak, av_t, gate_t):
    bsz, t, aw = aq.shape
    dh = aw // ATT_HEADS
    nb = t // BLOCK
    assert nb % 8 == 0 and AUG_ROWS + nb <= dh and 2 * SLOPE_PARTS <= AUG_ROWS
    slopes = LOG2E * jnp.exp2(-8.0 * jnp.arange(1, ATT_HEADS + 1, dtype=F32) / ATT_HEADS)
    parts = []
    for _ in range(SLOPE_PARTS):
        parts.append(slopes.astype(BF16).astype(F32))
        slopes = slopes - parts[-1]
    pair_spec = pl.BlockSpec((1, t, 2 * dh), lambda b, p: (b, 0, p))
    return pl.pallas_call(
        _moba_kernel,
        out_shape=jax.ShapeDtypeStruct((bsz, t, aw), BF16),
        grid=(bsz, ATT_HEADS // 2),
        in_specs=[pl.BlockSpec(memory_space=pltpu.SMEM),
                  pair_spec, pair_spec,
                  pl.BlockSpec((1, 2 * dh, t), lambda b, p: (b, p, 0)),
                  pl.BlockSpec((1, 2 * nb, t), lambda b, p: (b, p, 0))],
        out_specs=pair_spec,
        scratch_shapes=[pltpu.VMEM((2, t, 2 * dh), BF16), pltpu.VMEM((2, 2 * dh, t), BF16),
                        pltpu.VMEM((2, dh + BF16_ROWS, t), BF16)],
        compiler_params=pltpu.CompilerParams(
            dimension_semantics=("parallel", "parallel"), vmem_limit_bytes=VMEM_LIMIT),
        name="moba",
    )(jnp.stack(parts), aq, ak, av_t, gate_t)


def _ffn_kernel(x_ref, mh_ref, ma_ref, mod_ref, g2_ref, wo_ref, wg_ref, wu_ref, wd_ref, o_ref, *, ff_chunks):
    hw = mh_ref.shape[2]
    mod = mod_ref[0]
    mixed = _dot(mh_ref[0], wo_ref[0:hw]) + _dot(ma_ref[0], wo_ref[hw:])
    x1 = x_ref[0] + mod[2:3] * mixed
    xn = x1 * lax.rsqrt(jnp.mean(x1 * x1, axis=-1, keepdims=True) + EPS)
    hb = ((xn * g2_ref[...]) * (1.0 + mod[4:5]) + mod[3:4]).astype(BF16)
    tiles = wg_ref.shape[1] // MXU_TILE
    bounds = [-(-tiles * ci // ff_chunks) * MXU_TILE for ci in range(ff_chunks)] + [wg_ref.shape[1]]
    y = jnp.zeros_like(x1)
    for lo, hi in zip(bounds[:-1], bounds[1:]):
        act = _silu(_dot(hb, wg_ref[:, lo:hi])) * _dot(hb, wu_ref[:, lo:hi])
        y = y + _dot(act.astype(BF16), wd_ref[lo:hi])
    o_ref[0] = x1 + mod[5:6] * y


def _ffn(x, mh, ma, mod, g2, w_out, w_gate, w_up, w_down, *, tm=512, ff_chunks=2):
    bsz, t, d = x.shape
    tok = lambda w: pl.BlockSpec((1, tm, w), lambda b, i: (b, i, 0))
    const = lambda shape: pl.BlockSpec(shape, lambda b, i: (0,) * len(shape), pipeline_mode=pl.Buffered(1))
    return pl.pallas_call(
        functools.partial(_ffn_kernel, ff_chunks=ff_chunks),
        out_shape=jax.ShapeDtypeStruct((bsz, t, d), x.dtype),
        grid=(bsz, t // tm),
        in_specs=[tok(d), tok(mh.shape[2]), tok(ma.shape[2]),
                  pl.BlockSpec((1, 6, d), lambda b, i: (b, 0, 0)),
                  const((1, d)), const(w_out.shape), const(w_gate.shape), const(w_up.shape), const(w_down.shape)],
        out_specs=tok(d),
        compiler_params=pltpu.CompilerParams(
            dimension_semantics=("parallel", "parallel"), vmem_limit_bytes=VMEM_LIMIT),
        name="ffn",
    )(x, mh, ma, mod, g2.reshape(1, d), w_out.astype(BF16), w_gate.astype(BF16), w_up.astype(BF16),
      w_down.astype(BF16))


def kernel(x, c, w_ada, b_ada, norm1_g, w_in, lb_logits, hg_norm_g, q_norm_g, k_norm_g,
           w_out, norm2_g, w_gate, w_up, w_down):
    bsz, _, d = x.shape
    for l in range(w_ada.shape[0]):
        mod = _adaln(c, w_ada[l], b_ada[l]).reshape(bsz, 6, d)
        hq, kk, lf, hv, hg, aq, ak, av, gate = _inproj(
            x, mod, norm1_g[l], w_in[l], lb_logits, q_norm_g[l], k_norm_g[l], layer=l)
        o_hg = _hgrn(hq, kk, lf, hv, hg, hg_norm_g[l])
        o_att = _moba(aq, ak, av, gate)
        x = _ffn(x, o_hg, o_att, mod, norm2_g[l], w_out[l], w_gate[l], w_up[l], w_down[l])
    return x
```

```python
import functools

import jax
import jax.numpy as jnp
from jax import lax
from jax.experimental import pallas as pl
from jax.experimental.pallas import tpu as pltpu

F32 = jnp.float32
BF16 = jnp.bfloat16

HG_HEADS = 4
ATT_HEADS = 8
BLOCK = 256
TOPK = 3
HG_CHUNK = 256
EPS = 1e-6
NEG = -(2.0 ** 100)

LOG2E = 1.4426950408889634
MXU_TILE = 256

VMEM_LIMIT = 56 * 1024 * 1024


def _silu(t):
    return t * jax.nn.sigmoid(t)


def _dot(a, b):
    return jnp.dot(a, b, preferred_element_type=F32)


def _dot_nt(a, b):
    return lax.dot_general(a, b, (((1,), (1,)), ((), ())), preferred_element_type=F32)


def _dot_tn(a, b):
    return lax.dot_general(a, b, (((0,), (0,)), ((), ())), preferred_element_type=F32)


def _split2(t):
    hi = t.astype(BF16)
    lo = (t - hi.astype(F32)).astype(BF16)
    return hi, lo


def _dot3(a, b, dot=_dot):
    a_hi, a_lo = _split2(a)
    b_hi, b_lo = _split2(b)
    return (dot(a_hi, b_lo) + dot(a_lo, b_hi)) + dot(a_hi, b_hi)


def _adaln_kernel(c_ref, w_ref, b_ref, o_ref):
    o_ref[...] = _dot3(_silu(c_ref[...]), w_ref[...]) + b_ref[...]


def _adaln(c, w, b):
    bsz, d = c.shape
    n = w.shape[1]
    tn = 1024
    return pl.pallas_call(
        _adaln_kernel,
        out_shape=jax.ShapeDtypeStruct((bsz, n), F32),
        grid=(n // tn,),
        in_specs=[pl.BlockSpec((bsz, d), lambda j: (0, 0)),
                  pl.BlockSpec((d, tn), lambda j: (0, j)),
                  pl.BlockSpec((1, tn), lambda j: (0, j))],
        out_specs=pl.BlockSpec((bsz, tn), lambda j: (0, j)),
        name="adaln",
    )(c, w, b.reshape(1, n))


def _inproj_kernel(x_ref, mod_ref, g1_ref, w_ref, wvt_ref, lbl_ref, qg_ref, kg_ref, pool_ref,
                   f32_0, f32_1, f32_2, f32_3,
                   hq_ref, kk_ref, lf_ref, hv_ref, hg_ref, aq_ref, ak_ref, avt_ref, gate_ref,
                   bf16_0, bf16_1, bf16_2, bf16_3,
                   kmean_scr, *, tm, hw, aw, layer):
    i = pl.program_id(1)

    for src, dst in ((f32_0, bf16_0), (f32_1, bf16_1), (f32_2, bf16_2), (f32_3, bf16_3)):
        dst[...] = src[...].astype(BF16)

    dh = aw // ATT_HEADS
    nb = kmean_scr.shape[0]

    x = x_ref[0]
    xn = x * lax.rsqrt(jnp.mean(x * x, axis=-1, keepdims=True) + EPS)
    mod = mod_ref[0]
    h = (xn * g1_ref[...]) * (1.0 + mod[1:2]) + mod[0:1]
    hb = h.astype(BF16)

    def proj(lo, width):
        return _dot(hb, w_ref[:, lo:lo + width])

    def head_norm(t, gain):
        t2 = (t * t).astype(BF16)
        pw = pool_ref.shape[0]
        ms = jnp.concatenate([_dot(t2[:, c:c + pw], pool_ref[...]) for c in range(0, aw, pw)], axis=1)
        return t * lax.rsqrt(ms + EPS) * gain

    kn = head_norm(proj(4 * hw + aw, aw), kg_ref[...])
    ak_ref[0] = kn.astype(BF16)

    @pl.when(i == 0)
    def _():
        kmean_scr[...] = jnp.zeros_like(kmean_scr)

    per_tile = tm // BLOCK
    for s in range(per_tile):
        kmean_scr[pl.ds(i * per_tile + s, 1), :] = jnp.mean(
            kn[s * BLOCK:(s + 1) * BLOCK], axis=0, keepdims=True)

    qn = head_norm(proj(4 * hw, aw), qg_ref[...])
    aq_ref[0] = (qn * (dh ** -0.5 * LOG2E)).astype(BF16)
    kmean = kmean_scr[...]
    rows = lax.broadcasted_iota(jnp.int32, (ATT_HEADS * nb, aw), 0)
    cols = lax.broadcasted_iota(jnp.int32, (ATT_HEADS * nb, aw), 1)
    sel = jnp.where(rows // nb == cols // dh, jnp.concatenate([kmean] * ATT_HEADS, axis=0), 0.0)
    gate_ref[0] = _dot3(sel, qn, dot=_dot_nt)
    avt_ref[0] = _dot_nt(wvt_ref[...], hb).astype(BF16)

    hq_ref[0] = _silu(proj(0, hw)).astype(BF16)
    lbl = lbl_ref[...]
    e = jnp.exp(lbl - jnp.max(lbl, axis=0, keepdims=True))
    lb = jnp.sum(e[0:layer + 1], axis=0, keepdims=True) / jnp.sum(e, axis=0, keepdims=True)
    f = lb + (1.0 - lb) * jax.nn.sigmoid(proj(hw, hw))
    lf_ref[0] = jnp.log2(f)
    kk_ref[0] = (1.0 - f).astype(BF16)
    hv_ref[0] = proj(2 * hw, hw).astype(BF16)
    hg_ref[0] = _silu(proj(3 * hw, hw)).astype(BF16)


def _cast_rows(rows, steps):
    chunk = BF16_ROWS
    while chunk * steps < rows or rows % chunk:
        chunk += BF16_ROWS
    return chunk


def _inproj(x, mod, g1, w_in, lb_logits, qg, kg, later_weights, *, layer, tm=512):
    bsz, t, d = x.shape
    hw = lb_logits.shape[1]
    aw = (w_in.shape[1] - 4 * hw) // 3
    dh = aw // ATT_HEADS
    nb = t // BLOCK
    tiles = t // tm
    lanes = jnp.arange(MXU_TILE)
    pool = jnp.where(lanes[:, None] // dh == lanes[None, :] // dh, 1.0 / dh, 0.0).astype(BF16)
    tok = lambda w: pl.BlockSpec((1, tm, w), lambda b, i: (b, i, 0))
    const = lambda shape: pl.BlockSpec(shape, lambda b, i: (0,) * len(shape), pipeline_mode=pl.Buffered(1))
    bf = lambda w: jax.ShapeDtypeStruct((bsz, t, w), BF16)
    tok_t = lambda rows: pl.BlockSpec((1, rows, tm), lambda b, i: (b, 0, i))

    def cast_spec(w):
        rows = _cast_rows(w.shape[0], bsz * tiles)
        last = w.shape[0] // rows - 1
        return pl.BlockSpec((rows, w.shape[1]), lambda b, i: (jnp.minimum(b * tiles + i, last), 0))

    cast_specs = [cast_spec(w) for w in later_weights]
    w_bf = w_in.astype(BF16)
    main_cols = 4 * hw + 2 * aw
    outs = pl.pallas_call(
        functools.partial(_inproj_kernel, tm=tm, hw=hw, aw=aw, layer=layer),
        out_shape=[bf(hw), bf(hw), jax.ShapeDtypeStruct((bsz, t, hw), F32), bf(hw), bf(hw),
                   bf(aw), bf(aw), jax.ShapeDtypeStruct((bsz, aw, t), BF16),
                   jax.ShapeDtypeStruct((bsz, ATT_HEADS * nb, t), F32)]
                  + [jax.ShapeDtypeStruct(w.shape, BF16) for w in later_weights],
        grid=(bsz, tiles),
        in_specs=[tok(d),
                  pl.BlockSpec((1, 6, d), lambda b, i: (b, 0, 0)),
                  const((1, d)), const((d, main_cols)), const((aw, d)), const(lb_logits.shape),
                  const((1, aw)), const((1, aw)), const((MXU_TILE, MXU_TILE))] + cast_specs,
        out_specs=[tok(hw), tok(hw), tok(hw), tok(hw), tok(hw), tok(aw), tok(aw), tok_t(aw),
                   tok_t(ATT_HEADS * nb)] + cast_specs,
        scratch_shapes=[pltpu.VMEM((nb, aw), F32)],
        compiler_params=pltpu.CompilerParams(
            dimension_semantics=("arbitrary", "arbitrary"), vmem_limit_bytes=VMEM_LIMIT),
        name="inproj",
    )(x, mod, g1.reshape(1, d), w_bf, w_bf[:, main_cols:].T, lb_logits,
      jnp.tile(qg, ATT_HEADS).reshape(1, aw), jnp.tile(kg, ATT_HEADS).reshape(1, aw), pool, *later_weights)
    return outs[:9], outs[9:]


HG_LEVELS = (128, 64, 32, 16, 8)
HG_DIAG = 8
HG_UNROLL = 1
BF16_ROWS = 16


def _group_rows(b, first, step, rows):
    n = b.shape[0] // rows
    return jnp.concatenate(
        [jnp.broadcast_to(b[first + g * step:first + g * step + 1], (rows, b.shape[1])) for g in range(n)], axis=0)


def _blend_rows(q, kk, m):
    n = q.shape[0] // m
    return jnp.concatenate([(q if g % 2 else kk)[g * m:(g + 1) * m] for g in range(n)], axis=0)


def _hgrn_kernel(q_ref, k_ref, lf_ref, v_ref, g_ref, gain_ref, o_ref, st_ref):
    c_len = HG_CHUNK
    t_len, hw = q_ref.shape[1], q_ref.shape[2]
    d = hw // HG_HEADS
    r = lax.broadcasted_iota(jnp.int32, (c_len, c_len), 0)
    c = lax.broadcasted_iota(jnp.int32, (c_len, c_len), 1)
    tri = (r >= c).astype(BF16)
    level_masks = [(r // (2 * m) == c // (2 * m)) & ((r // m) % 2 == 1) & ((c // m) % 2 == 0) for m in HG_LEVELS]
    diag_mask = (r // HG_DIAG == c // HG_DIAG) & (r >= c)
    gain = gain_ref[...]

    st_ref[...] = jnp.zeros_like(st_ref)

    def body(ci, carry):
        sl = pl.ds(pl.multiple_of(ci * c_len, c_len), c_len)
        q = q_ref[0, sl, :]
        kk = k_ref[0, sl, :]
        lf = lf_ref[0, sl, :]
        v = v_ref[0, sl, :]

        l1 = lf.astype(BF16)
        r1 = lf - l1.astype(F32)
        l2 = r1.astype(BF16)
        l3 = (r1 - l2.astype(F32)).astype(BF16)
        b = (_dot(tri, l3) + _dot(tri, l2)) + _dot(tri, l1)
        b_last = b[c_len - 1:c_len]

        q_in = q * jnp.exp2(b).astype(BF16)
        k_st = kk * jnp.exp2(b_last - b).astype(BF16)
        decay = jnp.exp2(b_last)
        pairs = []
        for m in HG_LEVELS:
            w = jnp.exp2(-jnp.abs(b - _group_rows(b, m - 1, 2 * m, 2 * m)))
            if m % BF16_ROWS == 0:
                x = _blend_rows(q, kk, m) * w.astype(BF16)
            else:
                x = (_blend_rows(q.astype(F32), kk.astype(F32), m) * w).astype(BF16)
            pairs.append((x, x))
        dd = b - _group_rows(b, 0, HG_DIAG, HG_DIAG)
        pairs.append((q * jnp.exp2(dd).astype(BF16), kk * jnp.exp2(-dd).astype(BF16)))

        outs = []
        for h in range(HG_HEADS):
            hs = slice(h * d, (h + 1) * d)
            st = st_ref[h]
            o = _dot_nt(q_in[:, hs], st.astype(BF16))
            st_ref[h] = decay[:, hs] * st + _dot_tn(v[:, hs], k_st[:, hs])
            attn = jnp.zeros((c_len, c_len), F32)
            for (qw, kw), mask in zip(pairs, level_masks + [diag_mask]):
                attn = jnp.where(mask, _dot_nt(qw[:, hs], kw[:, hs]), attn)
            o = o + _dot(attn.astype(BF16), v[:, hs])
            outs.append(o * lax.rsqrt(jnp.mean(o * o, axis=-1, keepdims=True) + EPS) * gain)
        o_ref[0, sl, :] = (jnp.concatenate(outs, axis=1) * g_ref[0, sl, :].astype(F32)).astype(o_ref.dtype)
        return carry

    lax.fori_loop(0, t_len // c_len, body, 0, unroll=HG_UNROLL)


def _hgrn(hq, kk, lf, hv, hg, gain):
    bsz, t, hw = hq.shape
    d = hw // HG_HEADS
    blk = pl.BlockSpec((1, t, hw), lambda b: (b, 0, 0))
    return pl.pallas_call(
        _hgrn_kernel,
        out_shape=jax.ShapeDtypeStruct((bsz, t, hw), BF16),
        grid=(bsz,),
        in_specs=[blk, blk, blk, blk, blk, pl.BlockSpec((1, d), lambda b: (0, 0))],
        out_specs=blk,
        scratch_shapes=[pltpu.VMEM((HG_HEADS, d, d), F32)],
        compiler_params=pltpu.CompilerParams(dimension_semantics=("parallel",), vmem_limit_bytes=VMEM_LIMIT),
        name="hgrn",
    )(hq, kk, lf, hv, hg, gain.reshape(1, d))


SLOPE_PARTS = 3
AUG_ROWS = 8
QUERY_TILE = 2 * MXU_TILE


def _moba_kernel(slope_ref, q_ref, k_ref, vt_ref, gate_ref, o_ref, kaug_scr, qaug_scr, vaug_scr):
    p = pl.program_id(1)
    t_len, width = k_ref.shape[1], k_ref.shape[2]
    nb = t_len // BLOCK
    dh = width // 2

    k = k_ref[0]
    vt = vt_ref[0]
    qt = q_ref[0].astype(F32).T
    row = lax.broadcasted_iota(jnp.int32, k.shape, 0)
    ln = lax.broadcasted_iota(jnp.int32, k.shape, 1)
    in_block = (row % BLOCK).astype(F32)
    block_start = (row - row % BLOCK).astype(F32)
    feat = lax.broadcasted_iota(jnp.int32, (width, t_len), 0)
    jrow = lax.broadcasted_iota(jnp.int32, (nb, t_len), 0)
    qblk = lax.broadcasted_iota(jnp.int32, (nb, t_len), 1) // BLOCK
    crow = lax.broadcasted_iota(jnp.int32, (AUG_ROWS, t_len), 0)
    for a in range(2):
        x0 = (1 - a) * dh
        y0 = x0 + AUG_ROWS
        extra = jnp.where(ln < x0 + SLOPE_PARTS, in_block,
                          jnp.where(ln < x0 + 2 * SLOPE_PARTS, block_start,
                                    (ln - y0 == row // BLOCK).astype(F32)))
        kaug_scr[a] = jnp.where((ln >= x0) & (ln < y0 + nb), extra.astype(BF16), k)
        vaug_scr[a] = jnp.concatenate([vt[a * dh:(a + 1) * dh], jnp.ones((BF16_ROWS, t_len), BF16)], axis=0)

        g = jnp.where(jrow < qblk, gate_ref[0, a * nb:(a + 1) * nb, :], -jnp.inf)
        rank = jnp.zeros((nb, t_len), jnp.int32)
        for i in range(nb - 1):
            gi = g[i:i + 1, :]
            rank = rank + ((gi > g) | ((gi == g) & (i < jrow))).astype(jnp.int32)
        sel_t = jnp.where((rank < TOPK) | (jrow >= qblk), 0.0, NEG)

        slope_t = jnp.zeros((AUG_ROWS, t_len), F32)
        for i in range(SLOPE_PARTS):
            slope_t = jnp.where((crow == i) | (crow == SLOPE_PARTS + i), slope_ref[i, 2 * p + a], slope_t)
        pieces = [slope_t, sel_t, jnp.zeros((width - x0 - AUG_ROWS - nb, t_len), F32)]
        if x0:
            pieces.insert(0, jnp.zeros((x0, t_len), F32))
        own = (feat >= a * dh) & (feat < (a + 1) * dh)
        qaug_scr[a] = jnp.where(own, qt, jnp.concatenate(pieces, axis=0)).astype(BF16)

    key = lax.broadcasted_iota(jnp.int32, (QUERY_TILE, QUERY_TILE), 0)
    qry = lax.broadcasted_iota(jnp.int32, (QUERY_TILE, QUERY_TILE), 1)
    causal = key <= qry

    def head(lo, a):
        hi = lo + QUERY_TILE
        s = _dot(kaug_scr[a, 0:hi, :], qaug_scr[a, :, lo:hi])
        s_own = jnp.where(causal, s[lo:], NEG)
        s = s_own if lo == 0 else jnp.concatenate([s[:lo], s_own], axis=0)
        m = jnp.max(s, axis=0, keepdims=True)
        acc = _dot(vaug_scr[a, :, 0:hi], jnp.exp2(s - m).astype(BF16))
        return acc[0:dh] / acc[dh:dh + 1]

    for lo in range(0, t_len, QUERY_TILE):
        out_t = jnp.concatenate([head(lo, 0), head(lo, 1)], axis=0)
        o_ref[0, lo:lo + QUERY_TILE, :] = out_t.T.astype(o_ref.dtype)


def _moba(aq, ak, av_t, gate_t):
    bsz, t, aw = aq.shape
    dh = aw // ATT_HEADS
    nb = t // BLOCK
    assert nb % 8 == 0 and AUG_ROWS + nb <= dh and 2 * SLOPE_PARTS <= AUG_ROWS
    slopes = LOG2E * jnp.exp2(-8.0 * jnp.arange(1, ATT_HEADS + 1, dtype=F32) / ATT_HEADS)
    parts = []
    for _ in range(SLOPE_PARTS):
        parts.append(slopes.astype(BF16).astype(F32))
        slopes = slopes - parts[-1]
    pair_spec = pl.BlockSpec((1, t, 2 * dh), lambda b, p: (b, 0, p))
    return pl.pallas_call(
        _moba_kernel,
        out_shape=jax.ShapeDtypeStruct((bsz, t, aw), BF16),
        grid=(bsz, ATT_HEADS // 2),
        in_specs=[pl.BlockSpec(memory_space=pltpu.SMEM),
                  pair_spec, pair_spec,
                  pl.BlockSpec((1, 2 * dh, t), lambda b, p: (b, p, 0)),
                  pl.BlockSpec((1, 2 * nb, t), lambda b, p: (b, p, 0))],
        out_specs=pair_spec,
        scratch_shapes=[pltpu.VMEM((2, t, 2 * dh), BF16), pltpu.VMEM((2, 2 * dh, t), BF16),
                        pltpu.VMEM((2, dh + BF16_ROWS, t), BF16)],
        compiler_params=pltpu.CompilerParams(
            dimension_semantics=("parallel", "parallel"), vmem_limit_bytes=VMEM_LIMIT),
        name="moba",
    )(jnp.stack(parts), aq, ak, av_t, gate_t)


def _ffn_kernel(x_ref, mh_ref, ma_ref, mod_ref, g2_ref, wo_ref, wg_ref, wu_ref, wd_ref, o_ref, *, ff_chunks):
    hw = mh_ref.shape[2]
    mod = mod_ref[0]
    mixed = _dot(mh_ref[0], wo_ref[0:hw]) + _dot(ma_ref[0], wo_ref[hw:])
    x1 = x_ref[0] + mod[2:3] * mixed
    xn = x1 * lax.rsqrt(jnp.mean(x1 * x1, axis=-1, keepdims=True) + EPS)
    hb = ((xn * g2_ref[...]) * (1.0 + mod[4:5]) + mod[3:4]).astype(BF16)
    tiles = wg_ref.shape[1] // MXU_TILE
    bounds = [-(-tiles * ci // ff_chunks) * MXU_TILE for ci in range(ff_chunks)] + [wg_ref.shape[1]]
    y = jnp.zeros_like(x1)
    for lo, hi in zip(bounds[:-1], bounds[1:]):
        act = _silu(_dot(hb, wg_ref[:, lo:hi])) * _dot(hb, wu_ref[:, lo:hi])
        y = y + _dot(act.astype(BF16), wd_ref[lo:hi])
    o_ref[0] = x1 + mod[5:6] * y


def _ffn(x, mh, ma, mod, g2, w_out, w_gate, w_up, w_down, *, tm=512, ff_chunks=2):
    bsz, t, d = x.shape
    tok = lambda w: pl.BlockSpec((1, tm, w), lambda b, i: (b, i, 0))
    const = lambda shape: pl.BlockSpec(shape, lambda b, i: (0,) * len(shape), pipeline_mode=pl.Buffered(1))
    return pl.pallas_call(
        functools.partial(_ffn_kernel, ff_chunks=ff_chunks),
        out_shape=jax.ShapeDtypeStruct((bsz, t, d), x.dtype),
        grid=(bsz, t // tm),
        in_specs=[tok(d), tok(mh.shape[2]), tok(ma.shape[2]),
                  pl.BlockSpec((1, 6, d), lambda b, i: (b, 0, 0)),
                  const((1, d)), const(w_out.shape), const(w_gate.shape), const(w_up.shape), const(w_down.shape)],
        out_specs=tok(d),
        compiler_params=pltpu.CompilerParams(
            dimension_semantics=("parallel", "parallel"), vmem_limit_bytes=VMEM_LIMIT),
        name="ffn",
    )(x, mh, ma, mod, g2.reshape(1, d), w_out, w_gate, w_up, w_down)


def kernel(x, c, w_ada, b_ada, norm1_g, w_in, lb_logits, hg_norm_g, q_norm_g, k_norm_g,
           w_out, norm2_g, w_gate, w_up, w_down):
    bsz, _, d = x.shape
    for l in range(w_ada.shape[0]):
        mod = _adaln(c, w_ada[l], b_ada[l]).reshape(bsz, 6, d)
        (hq, kk, lf, hv, hg, aq, ak, av_t, gate_t), ffn_weights = _inproj(
            x, mod, norm1_g[l], w_in[l], lb_logits, q_norm_g[l], k_norm_g[l],
            (w_out[l], w_gate[l], w_up[l], w_down[l]), layer=l)
        o_hg = _hgrn(hq, kk, lf, hv, hg, hg_norm_g[l])
        o_att = _moba(aq, ak, av_t, gate_t)
        x = _ffn(x, o_hg, o_att, mod, norm2_g[l], *ffn_weights)
    return x
```

```python
import functools

import jax
import jax.numpy as jnp
from jax import lax
from jax.experimental import pallas as pl
from jax.experimental.pallas import tpu as pltpu

F32 = jnp.float32
BF16 = jnp.bfloat16

HG_HEADS = 4
ATT_HEADS = 8
BLOCK = 256
TOPK = 3
HG_CHUNK = 256
EPS = 1e-6
NEG = -(2.0 ** 100)

LOG2E = 1.4426950408889634
MXU_TILE = 256

VMEM_LIMIT = 56 * 1024 * 1024


def _silu(t):
    return t * jax.nn.sigmoid(t)


def _dot(a, b):
    return jnp.dot(a, b, preferred_element_type=F32)


def _dot_nt(a, b):
    return lax.dot_general(a, b, (((1,), (1,)), ((), ())), preferred_element_type=F32)


def _dot_tn(a, b):
    return lax.dot_general(a, b, (((0,), (0,)), ((), ())), preferred_element_type=F32)


def _split2(t):
    hi = t.astype(BF16)
    lo = (t - hi.astype(F32)).astype(BF16)
    return hi, lo


def _dot3(a, b, dot=_dot):
    a_hi, a_lo = _split2(a)
    b_hi, b_lo = _split2(b)
    return (dot(a_hi, b_lo) + dot(a_lo, b_hi)) + dot(a_hi, b_hi)


def _adaln_kernel(c_ref, w_ref, b_ref, o_ref):
    o_ref[...] = _dot3(_silu(c_ref[...]), w_ref[...]) + b_ref[...]


def _adaln(c, w, b):
    bsz, d = c.shape
    n = w.shape[1]
    tn = 1024
    return pl.pallas_call(
        _adaln_kernel,
        out_shape=jax.ShapeDtypeStruct((bsz, n), F32),
        grid=(n // tn,),
        in_specs=[pl.BlockSpec((bsz, d), lambda j: (0, 0)),
                  pl.BlockSpec((d, tn), lambda j: (0, j)),
                  pl.BlockSpec((1, tn), lambda j: (0, j))],
        out_specs=pl.BlockSpec((bsz, tn), lambda j: (0, j)),
        name="adaln",
    )(c, w, b.reshape(1, n))


def _inproj_kernel(x_ref, mod_ref, g1_ref, w_ref, lbl_ref, qg_ref, kg_ref, pool_ref,
                   f32_0, f32_1, f32_2, f32_3,
                   hq_ref, kk_ref, lf_ref, hv_ref, hg_ref, aq_ref, ak_ref, avt_ref, gate_ref,
                   bf16_0, bf16_1, bf16_2, bf16_3,
                   kmean_scr, *, tm, hw, aw, layer):
    i = pl.program_id(1)

    for src, dst in ((f32_0, bf16_0), (f32_1, bf16_1), (f32_2, bf16_2), (f32_3, bf16_3)):
        dst[...] = src[...].astype(BF16)

    dh = aw // ATT_HEADS
    nb = kmean_scr.shape[0]

    x = x_ref[0]
    xn = x * lax.rsqrt(jnp.mean(x * x, axis=-1, keepdims=True) + EPS)
    mod = mod_ref[0]
    h = (xn * g1_ref[...]) * (1.0 + mod[1:2]) + mod[0:1]
    hb = h.astype(BF16)

    def proj(lo, width):
        return _dot(hb, w_ref[:, lo:lo + width])

    def head_norm(t, gain):
        t2 = (t * t).astype(BF16)
        pw = pool_ref.shape[0]
        ms = jnp.concatenate([_dot(t2[:, c:c + pw], pool_ref[...]) for c in range(0, aw, pw)], axis=1)
        return t * lax.rsqrt(ms + EPS) * gain

    kn = head_norm(proj(4 * hw + aw, aw), kg_ref[...])
    ak_ref[0] = kn.astype(BF16)

    @pl.when(i == 0)
    def _():
        kmean_scr[...] = jnp.zeros_like(kmean_scr)

    per_tile = tm // BLOCK
    for s in range(per_tile):
        kmean_scr[pl.ds(i * per_tile + s, 1), :] = jnp.mean(
            kn[s * BLOCK:(s + 1) * BLOCK], axis=0, keepdims=True)

    qn = head_norm(proj(4 * hw, aw), qg_ref[...])
    aq_ref[0] = (qn * (dh ** -0.5 * LOG2E)).astype(BF16)
    kmean = kmean_scr[...]
    rows = lax.broadcasted_iota(jnp.int32, (ATT_HEADS * nb, aw), 0)
    cols = lax.broadcasted_iota(jnp.int32, (ATT_HEADS * nb, aw), 1)
    sel = jnp.where(rows // nb == cols // dh, jnp.concatenate([kmean] * ATT_HEADS, axis=0), 0.0)
    gate_ref[0] = _dot3(sel, qn, dot=_dot_nt)
    avt_ref[0] = proj(4 * hw + 2 * aw, aw).T.astype(BF16)

    hq_ref[0] = _silu(proj(0, hw)).astype(BF16)
    lbl = lbl_ref[...]
    e = jnp.exp(lbl - jnp.max(lbl, axis=0, keepdims=True))
    lb = jnp.sum(e[0:layer + 1], axis=0, keepdims=True) / jnp.sum(e, axis=0, keepdims=True)
    f = lb + (1.0 - lb) * jax.nn.sigmoid(proj(hw, hw))
    lf_ref[0] = jnp.log2(f)
    kk_ref[0] = (1.0 - f).astype(BF16)
    hv_ref[0] = proj(2 * hw, hw).astype(BF16)
    hg_ref[0] = _silu(proj(3 * hw, hw)).astype(BF16)


def _cast_rows(rows, steps):
    chunk = BF16_ROWS
    while chunk * steps < rows or rows % chunk:
        chunk += BF16_ROWS
    return chunk


def _inproj(x, mod, g1, w_in, lb_logits, qg, kg, later_weights, *, layer, tm=512):
    bsz, t, d = x.shape
    hw = lb_logits.shape[1]
    aw = (w_in.shape[1] - 4 * hw) // 3
    dh = aw // ATT_HEADS
    nb = t // BLOCK
    tiles = t // tm
    lanes = jnp.arange(MXU_TILE)
    pool = jnp.where(lanes[:, None] // dh == lanes[None, :] // dh, 1.0 / dh, 0.0).astype(BF16)
    tok = lambda w: pl.BlockSpec((1, tm, w), lambda b, i: (b, i, 0))
    const = lambda shape: pl.BlockSpec(shape, lambda b, i: (0,) * len(shape), pipeline_mode=pl.Buffered(1))
    bf = lambda w: jax.ShapeDtypeStruct((bsz, t, w), BF16)
    tok_t = lambda rows: pl.BlockSpec((1, rows, tm), lambda b, i: (b, 0, i))

    def cast_spec(w):
        rows = _cast_rows(w.shape[0], bsz * tiles)
        last = w.shape[0] // rows - 1
        return pl.BlockSpec((rows, w.shape[1]), lambda b, i: (jnp.minimum(b * tiles + i, last), 0))

    cast_specs = [cast_spec(w) for w in later_weights]
    outs = pl.pallas_call(
        functools.partial(_inproj_kernel, tm=tm, hw=hw, aw=aw, layer=layer),
        out_shape=[bf(hw), bf(hw), jax.ShapeDtypeStruct((bsz, t, hw), F32), bf(hw), bf(hw),
                   bf(aw), bf(aw), jax.ShapeDtypeStruct((bsz, aw, t), BF16),
                   jax.ShapeDtypeStruct((bsz, ATT_HEADS * nb, t), F32)]
                  + [jax.ShapeDtypeStruct(w.shape, BF16) for w in later_weights],
        grid=(bsz, tiles),
        in_specs=[tok(d),
                  pl.BlockSpec((1, 6, d), lambda b, i: (b, 0, 0)),
                  const((1, d)), const(w_in.shape), const(lb_logits.shape),
                  const((1, aw)), const((1, aw)), const((MXU_TILE, MXU_TILE))] + cast_specs,
        out_specs=[tok(hw), tok(hw), tok(hw), tok(hw), tok(hw), tok(aw), tok(aw), tok_t(aw),
                   tok_t(ATT_HEADS * nb)] + cast_specs,
        scratch_shapes=[pltpu.VMEM((nb, aw), F32)],
        compiler_params=pltpu.CompilerParams(
            dimension_semantics=("arbitrary", "arbitrary"), vmem_limit_bytes=VMEM_LIMIT),
        name="inproj",
    )(x, mod, g1.reshape(1, d), w_in.astype(BF16), lb_logits,
      jnp.tile(qg, ATT_HEADS).reshape(1, aw), jnp.tile(kg, ATT_HEADS).reshape(1, aw), pool, *later_weights)
    return outs[:9], outs[9:]


HG_LEVELS = (128, 64, 32, 16, 8)
HG_DIAG = 8
HG_UNROLL = 1
BF16_ROWS = 16


def _group_rows(b, first, step, rows):
    n = b.shape[0] // rows
    return jnp.concatenate(
        [jnp.broadcast_to(b[first + g * step:first + g * step + 1], (rows, b.shape[1])) for g in range(n)], axis=0)


def _blend_rows(q, kk, m):
    n = q.shape[0] // m
    return jnp.concatenate([(q if g % 2 else kk)[g * m:(g + 1) * m] for g in range(n)], axis=0)


def _hgrn_kernel(q_ref, k_ref, lf_ref, v_ref, g_ref, gain_ref, o_ref, st_ref):
    c_len = HG_CHUNK
    t_len, hw = q_ref.shape[1], q_ref.shape[2]
    d = hw // HG_HEADS
    r = lax.broadcasted_iota(jnp.int32, (c_len, c_len), 0)
    c = lax.broadcasted_iota(jnp.int32, (c_len, c_len), 1)
    tri = (r >= c).astype(BF16)
    level_masks = [(r // (2 * m) == c // (2 * m)) & ((r // m) % 2 == 1) & ((c // m) % 2 == 0) for m in HG_LEVELS]
    diag_mask = (r // HG_DIAG == c // HG_DIAG) & (r >= c)
    gain = gain_ref[...]

    st_ref[...] = jnp.zeros_like(st_ref)

    def body(ci, carry):
        sl = pl.ds(pl.multiple_of(ci * c_len, c_len), c_len)
        q = q_ref[0, sl, :]
        kk = k_ref[0, sl, :]
        lf = lf_ref[0, sl, :]
        v = v_ref[0, sl, :]

        l1 = lf.astype(BF16)
        r1 = lf - l1.astype(F32)
        l2 = r1.astype(BF16)
        l3 = (r1 - l2.astype(F32)).astype(BF16)
        b = (_dot(tri, l3) + _dot(tri, l2)) + _dot(tri, l1)
        b_last = b[c_len - 1:c_len]

        q_in = q * jnp.exp2(b).astype(BF16)
        k_st = kk * jnp.exp2(b_last - b).astype(BF16)
        decay = jnp.exp2(b_last)
        pairs = []
        for m in HG_LEVELS:
            w = jnp.exp2(-jnp.abs(b - _group_rows(b, m - 1, 2 * m, 2 * m)))
            if m % BF16_ROWS == 0:
                x = _blend_rows(q, kk, m) * w.astype(BF16)
            else:
                x = (_blend_rows(q.astype(F32), kk.astype(F32), m) * w).astype(BF16)
            pairs.append((x, x))
        dd = b - _group_rows(b, 0, HG_DIAG, HG_DIAG)
        pairs.append((q * jnp.exp2(dd).astype(BF16), kk * jnp.exp2(-dd).astype(BF16)))

        outs = []
        for h in range(HG_HEADS):
            hs = slice(h * d, (h + 1) * d)
            st = st_ref[h]
            o = _dot_nt(q_in[:, hs], st.astype(BF16))
            st_ref[h] = decay[:, hs] * st + _dot_tn(v[:, hs], k_st[:, hs])
            attn = jnp.zeros((c_len, c_len), F32)
            for (qw, kw), mask in zip(pairs, level_masks + [diag_mask]):
                attn = jnp.where(mask, _dot_nt(qw[:, hs], kw[:, hs]), attn)
            o = o + _dot(attn.astype(BF16), v[:, hs])
            outs.append(o * lax.rsqrt(jnp.mean(o * o, axis=-1, keepdims=True) + EPS) * gain)
        o_ref[0, sl, :] = (jnp.concatenate(outs, axis=1) * g_ref[0, sl, :].astype(F32)).astype(o_ref.dtype)
        return carry

    lax.fori_loop(0, t_len // c_len, body, 0, unroll=HG_UNROLL)


def _hgrn(hq, kk, lf, hv, hg, gain):
    bsz, t, hw = hq.shape
    d = hw // HG_HEADS
    blk = pl.BlockSpec((1, t, hw), lambda b: (b, 0, 0))
    return pl.pallas_call(
        _hgrn_kernel,
        out_shape=jax.ShapeDtypeStruct((bsz, t, hw), BF16),
        grid=(bsz,),
        in_specs=[blk, blk, blk, blk, blk, pl.BlockSpec((1, d), lambda b: (0, 0))],
        out_specs=blk,
        scratch_shapes=[pltpu.VMEM((HG_HEADS, d, d), F32)],
        compiler_params=pltpu.CompilerParams(dimension_semantics=("parallel",), vmem_limit_bytes=VMEM_LIMIT),
        name="hgrn",
    )(hq, kk, lf, hv, hg, gain.reshape(1, d))


SLOPE_PARTS = 3
AUG_ROWS = 8
QUERY_TILE = 2 * MXU_TILE


def _moba_kernel(slope_ref, q_ref, k_ref, vt_ref, gate_ref, o_ref, kaug_scr, qaug_scr, vaug_scr):
    p = pl.program_id(1)
    t_len, width = k_ref.shape[1], k_ref.shape[2]
    nb = t_len // BLOCK
    dh = width // 2

    k = k_ref[0]
    vt = vt_ref[0]
    qt = q_ref[0].astype(F32).T
    row = lax.broadcasted_iota(jnp.int32, k.shape, 0)
    ln = lax.broadcasted_iota(jnp.int32, k.shape, 1)
    in_block = (row % BLOCK).astype(F32)
    block_start = (row - row % BLOCK).astype(F32)
    feat = lax.broadcasted_iota(jnp.int32, (width, t_len), 0)
    jrow = lax.broadcasted_iota(jnp.int32, (nb, t_len), 0)
    qblk = lax.broadcasted_iota(jnp.int32, (nb, t_len), 1) // BLOCK
    crow = lax.broadcasted_iota(jnp.int32, (AUG_ROWS, t_len), 0)
    for a in range(2):
        x0 = (1 - a) * dh
        y0 = x0 + AUG_ROWS
        extra = jnp.where(ln < x0 + SLOPE_PARTS, in_block,
                          jnp.where(ln < x0 + 2 * SLOPE_PARTS, block_start,
                                    (ln - y0 == row // BLOCK).astype(F32)))
        kaug_scr[a] = jnp.where((ln >= x0) & (ln < y0 + nb), extra.astype(BF16), k)
        vaug_scr[a] = jnp.concatenate([vt[a * dh:(a + 1) * dh], jnp.ones((BF16_ROWS, t_len), BF16)], axis=0)

        g = jnp.where(jrow < qblk, gate_ref[0, a * nb:(a + 1) * nb, :], -jnp.inf)
        rank = jnp.zeros((nb, t_len), jnp.int32)
        for i in range(nb - 1):
            gi = g[i:i + 1, :]
            rank = rank + ((gi > g) | ((gi == g) & (i < jrow))).astype(jnp.int32)
        sel_t = jnp.where((rank < TOPK) | (jrow >= qblk), 0.0, NEG)

        slope_t = jnp.zeros((AUG_ROWS, t_len), F32)
        for i in range(SLOPE_PARTS):
            slope_t = jnp.where((crow == i) | (crow == SLOPE_PARTS + i), slope_ref[i, 2 * p + a], slope_t)
        pieces = [slope_t, sel_t, jnp.zeros((width - x0 - AUG_ROWS - nb, t_len), F32)]
        if x0:
            pieces.insert(0, jnp.zeros((x0, t_len), F32))
        own = (feat >= a * dh) & (feat < (a + 1) * dh)
        qaug_scr[a] = jnp.where(own, qt, jnp.concatenate(pieces, axis=0)).astype(BF16)

    key = lax.broadcasted_iota(jnp.int32, (QUERY_TILE, QUERY_TILE), 0)
    qry = lax.broadcasted_iota(jnp.int32, (QUERY_TILE, QUERY_TILE), 1)
    causal = key <= qry

    def head(lo, a):
        hi = lo + QUERY_TILE
        s = _dot(kaug_scr[a, 0:hi, :], qaug_scr[a, :, lo:hi])
        s_own = jnp.where(causal, s[lo:], NEG)
        s = s_own if lo == 0 else jnp.concatenate([s[:lo], s_own], axis=0)
        m = jnp.max(s, axis=0, keepdims=True)
        acc = _dot(vaug_scr[a, :, 0:hi], jnp.exp2(s - m).astype(BF16))
        return acc[0:dh] / acc[dh:dh + 1]

    for lo in range(0, t_len, QUERY_TILE):
        out_t = jnp.concatenate([head(lo, 0), head(lo, 1)], axis=0)
        o_ref[0, lo:lo + QUERY_TILE, :] = out_t.T.astype(o_ref.dtype)


def _moba(aq, ak, av_t, gate_t):
    bsz, t, aw = aq.shape
    dh = aw // ATT_HEADS
    nb = t // BLOCK
    assert nb % 8 == 0 and AUG_ROWS + nb <= dh and 2 * SLOPE_PARTS <= AUG_ROWS
    slopes = LOG2E * jnp.exp2(-8.0 * jnp.arange(1, ATT_HEADS + 1, dtype=F32) / ATT_HEADS)
    parts = []
    for _ in range(SLOPE_PARTS):
        parts.append(slopes.astype(BF16).astype(F32))
        slopes = slopes - parts[-1]
    pair_spec = pl.BlockSpec((1, t, 2 * dh), lambda b, p: (b, 0, p))
    return pl.pallas_call(
        _moba_kernel,
        out_shape=jax.ShapeDtypeStruct((bsz, t, aw), BF16),
        grid=(bsz, ATT_HEADS // 2),
        in_specs=[pl.BlockSpec(memory_space=pltpu.SMEM),
                  pair_spec, pair_spec,
                  pl.BlockSpec((1, 2 * dh, t), lambda b, p: (b, p, 0)),
                  pl.BlockSpec((1, 2 * nb, t), lambda b, p: (b, p, 0))],
        out_specs=pair_spec,
        scratch_shapes=[pltpu.VMEM((2, t, 2 * dh), BF16), pltpu.VMEM((2, 2 * dh, t), BF16),
                        pltpu.VMEM((2, dh + BF16_ROWS, t), BF16)],
        compiler_params=pltpu.CompilerParams(
            dimension_semantics=("parallel", "parallel"), vmem_limit_bytes=VMEM_LIMIT),
        name="moba",
    )(jnp.stack(parts), aq, ak, av_t, gate_t)


def _ffn_kernel(x_ref, mh_ref, ma_ref, mod_ref, g2_ref, wo_ref, wg_ref, wu_ref, wd_ref, o_ref, *, ff_chunks):
    hw = mh_ref.shape[2]
    mod = mod_ref[0]
    mixed = _dot(mh_ref[0], wo_ref[0:hw]) + _dot(ma_ref[0], wo_ref[hw:])
    x1 = x_ref[0] + mod[2:3] * mixed
    xn = x1 * lax.rsqrt(jnp.mean(x1 * x1, axis=-1, keepdims=True) + EPS)
    hb = ((xn * g2_ref[...]) * (1.0 + mod[4:5]) + mod[3:4]).astype(BF16)
    tiles = wg_ref.shape[1] // MXU_TILE
    bounds = [-(-tiles * ci // ff_chunks) * MXU_TILE for ci in range(ff_chunks)] + [wg_ref.shape[1]]
    y = jnp.zeros_like(x1)
    for lo, hi in zip(bounds[:-1], bounds[1:]):
        act = _silu(_dot(hb, wg_ref[:, lo:hi])) * _dot(hb, wu_ref[:, lo:hi])
        y = y + _dot(act.astype(BF16), wd_ref[lo:hi])
    o_ref[0] = x1 + mod[5:6] * y


def _ffn(x, mh, ma, mod, g2, w_out, w_gate, w_up, w_down, *, tm=512, ff_chunks=2):
    bsz, t, d = x.shape
    tok = lambda w: pl.BlockSpec((1, tm, w), lambda b, i: (b, i, 0))
    const = lambda shape: pl.BlockSpec(shape, lambda b, i: (0,) * len(shape), pipeline_mode=pl.Buffered(1))
    return pl.pallas_call(
        functools.partial(_ffn_kernel, ff_chunks=ff_chunks),
        out_shape=jax.ShapeDtypeStruct((bsz, t, d), x.dtype),
        grid=(bsz, t // tm),
        in_specs=[tok(d), tok(mh.shape[2]), tok(ma.shape[2]),
                  pl.BlockSpec((1, 6, d), lambda b, i: (b, 0, 0)),
                  const((1, d)), const(w_out.shape), const(w_gate.shape), const(w_up.shape), const(w_down.shape)],
        out_specs=tok(d),
        compiler_params=pltpu.CompilerParams(
            dimension_semantics=("parallel", "parallel"), vmem_limit_bytes=VMEM_LIMIT),
        name="ffn",
    )(x, mh, ma, mod, g2.reshape(1, d), w_out, w_gate, w_up, w_down)


def kernel(x, c, w_ada, b_ada, norm1_g, w_in, lb_logits, hg_norm_g, q_norm_g, k_norm_g,
           w_out, norm2_g, w_gate, w_up, w_down):
    bsz, _, d = x.shape
    for l in range(w_ada.shape[0]):
        mod = _adaln(c, w_ada[l], b_ada[l]).reshape(bsz, 6, d)
        (hq, kk, lf, hv, hg, aq, ak, av_t, gate_t), ffn_weights = _inproj(
            x, mod, norm1_g[l], w_in[l], lb_logits, q_norm_g[l], k_norm_g[l],
            (w_out[l], w_gate[l], w_up[l], w_down[l]), layer=l)
        o_hg = _hgrn(hq, kk, lf, hv, hg, hg_norm_g[l])
        o_att = _moba(aq, ak, av_t, gate_t)
        x = _ffn(x, o_hg, o_att, mod, norm2_g[l], *ffn_weights)
    return x
```

```python
import functools

import jax
import jax.numpy as jnp
from jax import lax
from jax.experimental import pallas as pl
from jax.experimental.pallas import tpu as pltpu

F32 = jnp.float32
BF16 = jnp.bfloat16

HG_HEADS = 4
ATT_HEADS = 8
BLOCK = 256
TOPK = 3
HG_CHUNK = 256
EPS = 1e-6
NEG = -(2.0 ** 100)

LOG2E = 1.4426950408889634
MXU_TILE = 256

VMEM_LIMIT = 56 * 1024 * 1024


def _silu(t):
    return t * jax.nn.sigmoid(t)


def _dot(a, b):
    return jnp.dot(a, b, preferred_element_type=F32)


def _dot_nt(a, b):
    return lax.dot_general(a, b, (((1,), (1,)), ((), ())), preferred_element_type=F32)


def _dot_tn(a, b):
    return lax.dot_general(a, b, (((0,), (0,)), ((), ())), preferred_element_type=F32)


def _split2(t):
    hi = t.astype(BF16)
    lo = (t - hi.astype(F32)).astype(BF16)
    return hi, lo


def _dot3(a, b, dot=_dot):
    a_hi, a_lo = _split2(a)
    b_hi, b_lo = _split2(b)
    return (dot(a_hi, b_lo) + dot(a_lo, b_hi)) + dot(a_hi, b_hi)


def _adaln_kernel(c_ref, w_ref, b_ref, o_ref):
    o_ref[...] = _dot3(_silu(c_ref[...]), w_ref[...]) + b_ref[...]


def _adaln(c, w, b):
    bsz, d = c.shape
    n = w.shape[1]
    tn = 1024
    return pl.pallas_call(
        _adaln_kernel,
        out_shape=jax.ShapeDtypeStruct((bsz, n), F32),
        grid=(n // tn,),
        in_specs=[pl.BlockSpec((bsz, d), lambda j: (0, 0)),
                  pl.BlockSpec((d, tn), lambda j: (0, j)),
                  pl.BlockSpec((1, tn), lambda j: (0, j))],
        out_specs=pl.BlockSpec((bsz, tn), lambda j: (0, j)),
        name="adaln",
    )(c, w, b.reshape(1, n))


def _inproj_kernel(x_ref, mod_ref, g1_ref, w_ref, lbl_ref, qg_ref, kg_ref, pool_ref,
                   f32_0, f32_1, f32_2, f32_3,
                   hq_ref, kk_ref, lf_ref, hv_ref, hg_ref, aq_ref, ak_ref, avt_ref, gate_ref,
                   bf16_0, bf16_1, bf16_2, bf16_3,
                   kmean_scr, *, tm, hw, aw, layer):
    i = pl.program_id(1)

    for src, dst in ((f32_0, bf16_0), (f32_1, bf16_1), (f32_2, bf16_2), (f32_3, bf16_3)):
        dst[...] = src[...].astype(BF16)

    dh = aw // ATT_HEADS
    nb = kmean_scr.shape[0]

    x = x_ref[0]
    xn = x * lax.rsqrt(jnp.mean(x * x, axis=-1, keepdims=True) + EPS)
    mod = mod_ref[0]
    h = (xn * g1_ref[...]) * (1.0 + mod[1:2]) + mod[0:1]
    hb = h.astype(BF16)

    def proj(lo, width):
        return _dot(hb, w_ref[:, lo:lo + width])

    def head_norm(t, gain):
        t2 = (t * t).astype(BF16)
        pw = pool_ref.shape[0]
        ms = jnp.concatenate([_dot(t2[:, c:c + pw], pool_ref[...]) for c in range(0, aw, pw)], axis=1)
        return t * lax.rsqrt(ms + EPS) * gain

    kn = head_norm(proj(4 * hw + aw, aw), kg_ref[...])
    ak_ref[0] = kn.astype(BF16)

    @pl.when(i == 0)
    def _():
        kmean_scr[...] = jnp.zeros_like(kmean_scr)

    per_tile = tm // BLOCK
    for s in range(per_tile):
        kmean_scr[pl.ds(i * per_tile + s, 1), :] = jnp.mean(
            kn[s * BLOCK:(s + 1) * BLOCK], axis=0, keepdims=True)

    qn = head_norm(proj(4 * hw, aw), qg_ref[...])
    aq_ref[0] = (qn * (dh ** -0.5 * LOG2E)).astype(BF16)
    kmean = kmean_scr[...]
    rows = lax.broadcasted_iota(jnp.int32, (ATT_HEADS * nb, aw), 0)
    cols = lax.broadcasted_iota(jnp.int32, (ATT_HEADS * nb, aw), 1)
    sel = jnp.where(rows // nb == cols // dh, jnp.concatenate([kmean] * ATT_HEADS, axis=0), 0.0)
    gate_ref[0] = _dot3(sel, qn, dot=_dot_nt)
    avt_ref[0] = proj(4 * hw + 2 * aw, aw).T.astype(BF16)

    hq_ref[0] = _silu(proj(0, hw)).astype(BF16)
    lbl = lbl_ref[...]
    e = jnp.exp(lbl - jnp.max(lbl, axis=0, keepdims=True))
    lb = jnp.sum(e[0:layer + 1], axis=0, keepdims=True) / jnp.sum(e, axis=0, keepdims=True)
    f = lb + (1.0 - lb) * jax.nn.sigmoid(proj(hw, hw))
    lf_ref[0] = jnp.log2(f)
    kk_ref[0] = (1.0 - f).astype(BF16)
    hv_ref[0] = proj(2 * hw, hw).astype(BF16)
    hg_ref[0] = _silu(proj(3 * hw, hw)).astype(BF16)


def _cast_rows(rows, steps):
    chunk = BF16_ROWS
    while chunk * steps < rows or rows % chunk:
        chunk += BF16_ROWS
    return chunk


def _inproj(x, mod, g1, w_in, lb_logits, qg, kg, later_weights, *, layer, tm=1024):
    bsz, t, d = x.shape
    hw = lb_logits.shape[1]
    aw = (w_in.shape[1] - 4 * hw) // 3
    dh = aw // ATT_HEADS
    nb = t // BLOCK
    tiles = t // tm
    lanes = jnp.arange(MXU_TILE)
    pool = jnp.where(lanes[:, None] // dh == lanes[None, :] // dh, 1.0 / dh, 0.0).astype(BF16)
    tok = lambda w: pl.BlockSpec((1, tm, w), lambda b, i: (b, i, 0))
    const = lambda shape: pl.BlockSpec(shape, lambda b, i: (0,) * len(shape), pipeline_mode=pl.Buffered(1))
    bf = lambda w: jax.ShapeDtypeStruct((bsz, t, w), BF16)
    tok_t = lambda rows: pl.BlockSpec((1, rows, tm), lambda b, i: (b, 0, i))

    def cast_spec(w):
        rows = _cast_rows(w.shape[0], bsz * tiles)
        last = w.shape[0] // rows - 1
        return pl.BlockSpec((rows, w.shape[1]), lambda b, i: (jnp.minimum(b * tiles + i, last), 0))

    cast_specs = [cast_spec(w) for w in later_weights]
    outs = pl.pallas_call(
        functools.partial(_inproj_kernel, tm=tm, hw=hw, aw=aw, layer=layer),
        out_shape=[bf(hw), bf(hw), jax.ShapeDtypeStruct((bsz, t, hw), F32), bf(hw), bf(hw),
                   bf(aw), bf(aw), jax.ShapeDtypeStruct((bsz, aw, t), BF16),
                   jax.ShapeDtypeStruct((bsz, ATT_HEADS * nb, t), F32)]
                  + [jax.ShapeDtypeStruct(w.shape, BF16) for w in later_weights],
        grid=(bsz, tiles),
        in_specs=[tok(d),
                  pl.BlockSpec((1, 6, d), lambda b, i: (b, 0, 0)),
                  const((1, d)), const(w_in.shape), const(lb_logits.shape),
                  const((1, aw)), const((1, aw)), const((MXU_TILE, MXU_TILE))] + cast_specs,
        out_specs=[tok(hw), tok(hw), tok(hw), tok(hw), tok(hw), tok(aw), tok(aw), tok_t(aw),
                   tok_t(ATT_HEADS * nb)] + cast_specs,
        scratch_shapes=[pltpu.VMEM((nb, aw), F32)],
        compiler_params=pltpu.CompilerParams(
            dimension_semantics=("arbitrary", "arbitrary"), vmem_limit_bytes=VMEM_LIMIT),
        name="inproj",
    )(x, mod, g1.reshape(1, d), w_in.astype(BF16), lb_logits,
      jnp.tile(qg, ATT_HEADS).reshape(1, aw), jnp.tile(kg, ATT_HEADS).reshape(1, aw), pool, *later_weights)
    return outs[:9], outs[9:]


HG_LEVELS = (128, 64, 32, 16, 8)
HG_DIAG = 8
HG_UNROLL = 1
BF16_ROWS = 16


def _group_rows(b, first, step, rows):
    n = b.shape[0] // rows
    return jnp.concatenate(
        [jnp.broadcast_to(b[first + g * step:first + g * step + 1], (rows, b.shape[1])) for g in range(n)], axis=0)


def _blend_rows(q, kk, m):
    n = q.shape[0] // m
    return jnp.concatenate([(q if g % 2 else kk)[g * m:(g + 1) * m] for g in range(n)], axis=0)


def _hgrn_kernel(q_ref, k_ref, lf_ref, v_ref, g_ref, gain_ref, o_ref, st_ref):
    c_len = HG_CHUNK
    t_len, hw = q_ref.shape[1], q_ref.shape[2]
    d = hw // HG_HEADS
    r = lax.broadcasted_iota(jnp.int32, (c_len, c_len), 0)
    c = lax.broadcasted_iota(jnp.int32, (c_len, c_len), 1)
    tri = (r >= c).astype(BF16)
    level_masks = [(r // (2 * m) == c // (2 * m)) & ((r // m) % 2 == 1) & ((c // m) % 2 == 0) for m in HG_LEVELS]
    diag_mask = (r // HG_DIAG == c // HG_DIAG) & (r >= c)
    gain = gain_ref[...]

    st_ref[...] = jnp.zeros_like(st_ref)

    def body(ci, carry):
        sl = pl.ds(pl.multiple_of(ci * c_len, c_len), c_len)
        q = q_ref[0, sl, :]
        kk = k_ref[0, sl, :]
        lf = lf_ref[0, sl, :]
        v = v_ref[0, sl, :]

        l1 = lf.astype(BF16)
        r1 = lf - l1.astype(F32)
        l2 = r1.astype(BF16)
        l3 = (r1 - l2.astype(F32)).astype(BF16)
        b = (_dot(tri, l3) + _dot(tri, l2)) + _dot(tri, l1)
        b_last = b[c_len - 1:c_len]

        q_in = q * jnp.exp2(b).astype(BF16)
        k_st = kk * jnp.exp2(b_last - b).astype(BF16)
        decay = jnp.exp2(b_last)
        pairs = []
        for m in HG_LEVELS:
            w = jnp.exp2(-jnp.abs(b - _group_rows(b, m - 1, 2 * m, 2 * m)))
            if m % BF16_ROWS == 0:
                x = _blend_rows(q, kk, m) * w.astype(BF16)
            else:
                x = (_blend_rows(q.astype(F32), kk.astype(F32), m) * w).astype(BF16)
            pairs.append((x, x))
        dd = b - _group_rows(b, 0, HG_DIAG, HG_DIAG)
        pairs.append((q * jnp.exp2(dd).astype(BF16), kk * jnp.exp2(-dd).astype(BF16)))

        outs = []
        for h in range(HG_HEADS):
            hs = slice(h * d, (h + 1) * d)
            st = st_ref[h]
            o = _dot_nt(q_in[:, hs], st.astype(BF16))
            st_ref[h] = decay[:, hs] * st + _dot_tn(v[:, hs], k_st[:, hs])
            attn = jnp.zeros((c_len, c_len), F32)
            for (qw, kw), mask in zip(pairs, level_masks + [diag_mask]):
                attn = jnp.where(mask, _dot_nt(qw[:, hs], kw[:, hs]), attn)
            o = o + _dot(attn.astype(BF16), v[:, hs])
            outs.append(o * lax.rsqrt(jnp.mean(o * o, axis=-1, keepdims=True) + EPS) * gain)
        o_ref[0, sl, :] = (jnp.concatenate(outs, axis=1) * g_ref[0, sl, :].astype(F32)).astype(o_ref.dtype)
        return carry

    lax.fori_loop(0, t_len // c_len, body, 0, unroll=HG_UNROLL)


def _hgrn(hq, kk, lf, hv, hg, gain):
    bsz, t, hw = hq.shape
    d = hw // HG_HEADS
    blk = pl.BlockSpec((1, t, hw), lambda b: (b, 0, 0))
    return pl.pallas_call(
        _hgrn_kernel,
        out_shape=jax.ShapeDtypeStruct((bsz, t, hw), BF16),
        grid=(bsz,),
        in_specs=[blk, blk, blk, blk, blk, pl.BlockSpec((1, d), lambda b: (0, 0))],
        out_specs=blk,
        scratch_shapes=[pltpu.VMEM((HG_HEADS, d, d), F32)],
        compiler_params=pltpu.CompilerParams(dimension_semantics=("parallel",), vmem_limit_bytes=VMEM_LIMIT),
        name="hgrn",
    )(hq, kk, lf, hv, hg, gain.reshape(1, d))


SLOPE_PARTS = 3
AUG_ROWS = 8
QUERY_TILE = 2 * MXU_TILE


def _moba_kernel(slope_ref, q_ref, k_ref, vt_ref, gate_ref, o_ref, kaug_scr, qaug_scr, vaug_scr):
    p = pl.program_id(1)
    t_len, width = k_ref.shape[1], k_ref.shape[2]
    nb = t_len // BLOCK
    dh = width // 2

    k = k_ref[0]
    vt = vt_ref[0]
    qt = q_ref[0].astype(F32).T
    row = lax.broadcasted_iota(jnp.int32, k.shape, 0)
    ln = lax.broadcasted_iota(jnp.int32, k.shape, 1)
    in_block = (row % BLOCK).astype(F32)
    block_start = (row - row % BLOCK).astype(F32)
    feat = lax.broadcasted_iota(jnp.int32, (width, t_len), 0)
    jrow = lax.broadcasted_iota(jnp.int32, (nb, t_len), 0)
    qblk = lax.broadcasted_iota(jnp.int32, (nb, t_len), 1) // BLOCK
    crow = lax.broadcasted_iota(jnp.int32, (AUG_ROWS, t_len), 0)
    for a in range(2):
        x0 = (1 - a) * dh
        y0 = x0 + AUG_ROWS
        extra = jnp.where(ln < x0 + SLOPE_PARTS, in_block,
                          jnp.where(ln < x0 + 2 * SLOPE_PARTS, block_start,
                                    (ln - y0 == row // BLOCK).astype(F32)))
        kaug_scr[a] = jnp.where((ln >= x0) & (ln < y0 + nb), extra.astype(BF16), k)
        vaug_scr[a] = jnp.concatenate([vt[a * dh:(a + 1) * dh], jnp.ones((BF16_ROWS, t_len), BF16)], axis=0)

        g = jnp.where(jrow < qblk, gate_ref[0, a * nb:(a + 1) * nb, :], -jnp.inf)
        rank = jnp.zeros((nb, t_len), jnp.int32)
        for i in range(nb - 1):
            gi = g[i:i + 1, :]
            rank = rank + ((gi > g) | ((gi == g) & (i < jrow))).astype(jnp.int32)
        sel_t = jnp.where((rank < TOPK) | (jrow >= qblk), 0.0, NEG)

        slope_t = jnp.zeros((AUG_ROWS, t_len), F32)
        for i in range(SLOPE_PARTS):
            slope_t = jnp.where((crow == i) | (crow == SLOPE_PARTS + i), slope_ref[i, 2 * p + a], slope_t)
        pieces = [slope_t, sel_t, jnp.zeros((width - x0 - AUG_ROWS - nb, t_len), F32)]
        if x0:
            pieces.insert(0, jnp.zeros((x0, t_len), F32))
        own = (feat >= a * dh) & (feat < (a + 1) * dh)
        qaug_scr[a] = jnp.where(own, qt, jnp.concatenate(pieces, axis=0)).astype(BF16)

    key = lax.broadcasted_iota(jnp.int32, (QUERY_TILE, QUERY_TILE), 0)
    qry = lax.broadcasted_iota(jnp.int32, (QUERY_TILE, QUERY_TILE), 1)
    causal = key <= qry

    def head(lo, a):
        hi = lo + QUERY_TILE
        s = _dot(kaug_scr[a, 0:hi, :], qaug_scr[a, :, lo:hi])
        s_own = jnp.where(causal, s[lo:], NEG)
        s = s_own if lo == 0 else jnp.concatenate([s[:lo], s_own], axis=0)
        m = jnp.max(s, axis=0, keepdims=True)
        acc = _dot(vaug_scr[a, :, 0:hi], jnp.exp2(s - m).astype(BF16))
        return acc[0:dh] / acc[dh:dh + 1]

    for lo in range(0, t_len, QUERY_TILE):
        out_t = jnp.concatenate([head(lo, 0), head(lo, 1)], axis=0)
        o_ref[0, lo:lo + QUERY_TILE, :] = out_t.T.astype(o_ref.dtype)


def _moba(aq, ak, av_t, gate_t):
    bsz, t, aw = aq.shape
    dh = aw // ATT_HEADS
    nb = t // BLOCK
    assert nb % 8 == 0 and AUG_ROWS + nb <= dh and 2 * SLOPE_PARTS <= AUG_ROWS
    slopes = LOG2E * jnp.exp2(-8.0 * jnp.arange(1, ATT_HEADS + 1, dtype=F32) / ATT_HEADS)
    parts = []
    for _ in range(SLOPE_PARTS):
        parts.append(slopes.astype(BF16).astype(F32))
        slopes = slopes - parts[-1]
    pair_spec = pl.BlockSpec((1, t, 2 * dh), lambda b, p: (b, 0, p))
    return pl.pallas_call(
        _moba_kernel,
        out_shape=jax.ShapeDtypeStruct((bsz, t, aw), BF16),
        grid=(bsz, ATT_HEADS // 2),
        in_specs=[pl.BlockSpec(memory_space=pltpu.SMEM),
                  pair_spec, pair_spec,
                  pl.BlockSpec((1, 2 * dh, t), lambda b, p: (b, p, 0)),
                  pl.BlockSpec((1, 2 * nb, t), lambda b, p: (b, p, 0))],
        out_specs=pair_spec,
        scratch_shapes=[pltpu.VMEM((2, t, 2 * dh), BF16), pltpu.VMEM((2, 2 * dh, t), BF16),
                        pltpu.VMEM((2, dh + BF16_ROWS, t), BF16)],
        compiler_params=pltpu.CompilerParams(
            dimension_semantics=("parallel", "parallel"), vmem_limit_bytes=VMEM_LIMIT),
        name="moba",
    )(jnp.stack(parts), aq, ak, av_t, gate_t)


def _ffn_kernel(x_ref, mh_ref, ma_ref, mod_ref, g2_ref, wo_ref, wg_ref, wu_ref, wd_ref, o_ref, *, ff_chunks):
    hw = mh_ref.shape[2]
    mod = mod_ref[0]
    mixed = _dot(mh_ref[0], wo_ref[0:hw]) + _dot(ma_ref[0], wo_ref[hw:])
    x1 = x_ref[0] + mod[2:3] * mixed
    xn = x1 * lax.rsqrt(jnp.mean(x1 * x1, axis=-1, keepdims=True) + EPS)
    hb = ((xn * g2_ref[...]) * (1.0 + mod[4:5]) + mod[3:4]).astype(BF16)
    tiles = wg_ref.shape[1] // MXU_TILE
    bounds = [-(-tiles * ci // ff_chunks) * MXU_TILE for ci in range(ff_chunks)] + [wg_ref.shape[1]]
    y = jnp.zeros_like(x1)
    for lo, hi in zip(bounds[:-1], bounds[1:]):
        act = _silu(_dot(hb, wg_ref[:, lo:hi])) * _dot(hb, wu_ref[:, lo:hi])
        y = y + _dot(act.astype(BF16), wd_ref[lo:hi])
    o_ref[0] = x1 + mod[5:6] * y


def _ffn(x, mh, ma, mod, g2, w_out, w_gate, w_up, w_down, *, tm=512, ff_chunks=2):
    bsz, t, d = x.shape
    tok = lambda w: pl.BlockSpec((1, tm, w), lambda b, i: (b, i, 0))
    const = lambda shape: pl.BlockSpec(shape, lambda b, i: (0,) * len(shape), pipeline_mode=pl.Buffered(1))
    return pl.pallas_call(
        functools.partial(_ffn_kernel, ff_chunks=ff_chunks),
        out_shape=jax.ShapeDtypeStruct((bsz, t, d), x.dtype),
        grid=(bsz, t // tm),
        in_specs=[tok(d), tok(mh.shape[2]), tok(ma.shape[2]),
                  pl.BlockSpec((1, 6, d), lambda b, i: (b, 0, 0)),
                  const((1, d)), const(w_out.shape), const(w_gate.shape), const(w_up.shape), const(w_down.shape)],
        out_specs=tok(d),
        compiler_params=pltpu.CompilerParams(
            dimension_semantics=("parallel", "parallel"), vmem_limit_bytes=VMEM_LIMIT),
        name="ffn",
    )(x, mh, ma, mod, g2.reshape(1, d), w_out, w_gate, w_up, w_down)


def kernel(x, c, w_ada, b_ada, norm1_g, w_in, lb_logits, hg_norm_g, q_norm_g, k_norm_g,
           w_out, norm2_g, w_gate, w_up, w_down):
    bsz, _, d = x.shape
    for l in range(w_ada.shape[0]):
        mod = _adaln(c, w_ada[l], b_ada[l]).reshape(bsz, 6, d)
        (hq, kk, lf, hv, hg, aq, ak, av_t, gate_t), ffn_weights = _inproj(
            x, mod, norm1_g[l], w_in[l], lb_logits, q_norm_g[l], k_norm_g[l],
            (w_out[l], w_gate[l], w_up[l], w_down[l]), layer=l)
        o_hg = _hgrn(hq, kk, lf, hv, hg, hg_norm_g[l])
        o_att = _moba(aq, ak, av_t, gate_t)
        x = _ffn(x, o_hg, o_att, mod, norm2_g[l], *ffn_weights)
    return x
```

```python
import functools

import jax
import jax.numpy as jnp
from jax import lax
from jax.experimental import pallas as pl
from jax.experimental.pallas import tpu as pltpu

F32 = jnp.float32
BF16 = jnp.bfloat16

HG_HEADS = 4
ATT_HEADS = 8
BLOCK = 256
TOPK = 3
HG_CHUNK = 256
EPS = 1e-6
NEG = -(2.0 ** 100)

LOG2E = 1.4426950408889634
MXU_TILE = 256

VMEM_LIMIT = 56 * 1024 * 1024


def _silu(t):
    return t * jax.nn.sigmoid(t)


def _dot(a, b):
    return jnp.dot(a, b, preferred_element_type=F32)


def _dot_nt(a, b):
    return lax.dot_general(a, b, (((1,), (1,)), ((), ())), preferred_element_type=F32)


def _dot_tn(a, b):
    return lax.dot_general(a, b, (((0,), (0,)), ((), ())), preferred_element_type=F32)


def _split2(t):
    hi = t.astype(BF16)
    lo = (t - hi.astype(F32)).astype(BF16)
    return hi, lo


def _dot3(a, b, dot=_dot):
    a_hi, a_lo = _split2(a)
    b_hi, b_lo = _split2(b)
    return (dot(a_hi, b_lo) + dot(a_lo, b_hi)) + dot(a_hi, b_hi)


def _adaln_kernel(c_ref, w_ref, b_ref, o_ref):
    o_ref[...] = _dot3(_silu(c_ref[...]), w_ref[...]) + b_ref[...]


def _adaln(c, w, b):
    bsz, d = c.shape
    n = w.shape[1]
    tn = 1024
    return pl.pallas_call(
        _adaln_kernel,
        out_shape=jax.ShapeDtypeStruct((bsz, n), F32),
        grid=(n // tn,),
        in_specs=[pl.BlockSpec((bsz, d), lambda j: (0, 0)),
                  pl.BlockSpec((d, tn), lambda j: (0, j)),
                  pl.BlockSpec((1, tn), lambda j: (0, j))],
        out_specs=pl.BlockSpec((bsz, tn), lambda j: (0, j)),
        name="adaln",
    )(c, w, b.reshape(1, n))


def _inproj_kernel(x_ref, mod_ref, g1_ref, w_ref, lbl_ref, qg_ref, kg_ref, pool_ref,
                   f32_0, f32_1, f32_2, f32_3,
                   hq_ref, kk_ref, lf_ref, hv_ref, hg_ref, aq_ref, ak_ref, avt_ref, gate_ref,
                   bf16_0, bf16_1, bf16_2, bf16_3,
                   kmean_scr, *, tm, hw, aw, layer):
    i = pl.program_id(1)

    for src, dst in ((f32_0, bf16_0), (f32_1, bf16_1), (f32_2, bf16_2), (f32_3, bf16_3)):
        dst[...] = src[...].astype(BF16)

    dh = aw // ATT_HEADS
    nb = kmean_scr.shape[0]

    x = x_ref[0]
    xn = x * lax.rsqrt(jnp.mean(x * x, axis=-1, keepdims=True) + EPS)
    mod = mod_ref[0]
    h = (xn * g1_ref[...]) * (1.0 + mod[1:2]) + mod[0:1]
    hb = h.astype(BF16)

    def proj(lo, width):
        return _dot(hb, w_ref[:, lo:lo + width])

    def head_norm(t, gain):
        t2 = (t * t).astype(BF16)
        pw = pool_ref.shape[0]
        ms = jnp.concatenate([_dot(t2[:, c:c + pw], pool_ref[...]) for c in range(0, aw, pw)], axis=1)
        return t * lax.rsqrt(ms + EPS) * gain

    kn = head_norm(proj(4 * hw + aw, aw), kg_ref[...])
    ak_ref[0] = kn.astype(BF16)

    @pl.when(i == 0)
    def _():
        kmean_scr[...] = jnp.zeros_like(kmean_scr)

    per_tile = tm // BLOCK
    for s in range(per_tile):
        kmean_scr[pl.ds(i * per_tile + s, 1), :] = jnp.mean(
            kn[s * BLOCK:(s + 1) * BLOCK], axis=0, keepdims=True)

    qn = head_norm(proj(4 * hw, aw), qg_ref[...])
    aq_ref[0] = (qn * (dh ** -0.5 * LOG2E)).astype(BF16)
    kmean = kmean_scr[...]
    rows = lax.broadcasted_iota(jnp.int32, (ATT_HEADS * nb, aw), 0)
    cols = lax.broadcasted_iota(jnp.int32, (ATT_HEADS * nb, aw), 1)
    sel = jnp.where(rows // nb == cols // dh, jnp.concatenate([kmean] * ATT_HEADS, axis=0), 0.0)
    gate_ref[0] = _dot3(sel, qn, dot=_dot_nt)
    avt_ref[0] = proj(4 * hw + 2 * aw, aw).T.astype(BF16)

    hq_ref[0] = _silu(proj(0, hw)).astype(BF16)
    lbl = lbl_ref[...]
    e = jnp.exp(lbl - jnp.max(lbl, axis=0, keepdims=True))
    lb = jnp.sum(e[0:layer + 1], axis=0, keepdims=True) / jnp.sum(e, axis=0, keepdims=True)
    f = lb + (1.0 - lb) * jax.nn.sigmoid(proj(hw, hw))
    lf_ref[0] = jnp.log2(f)
    kk_ref[0] = (1.0 - f).astype(BF16)
    hv_ref[0] = proj(2 * hw, hw).astype(BF16)
    hg_ref[0] = _silu(proj(3 * hw, hw)).astype(BF16)


def _cast_rows(rows, steps):
    chunk = BF16_ROWS
    while chunk * steps < rows or rows % chunk:
        chunk += BF16_ROWS
    return chunk


def _inproj(x, mod, g1, w_in, lb_logits, qg, kg, later_weights, *, layer, tm=512):
    bsz, t, d = x.shape
    hw = lb_logits.shape[1]
    aw = (w_in.shape[1] - 4 * hw) // 3
    dh = aw // ATT_HEADS
    nb = t // BLOCK
    tiles = t // tm
    lanes = jnp.arange(MXU_TILE)
    pool = jnp.where(lanes[:, None] // dh == lanes[None, :] // dh, 1.0 / dh, 0.0).astype(BF16)
    tok = lambda w: pl.BlockSpec((1, tm, w), lambda b, i: (b, i, 0))
    const = lambda shape: pl.BlockSpec(shape, lambda b, i: (0,) * len(shape), pipeline_mode=pl.Buffered(1))
    bf = lambda w: jax.ShapeDtypeStruct((bsz, t, w), BF16)
    tok_t = lambda rows: pl.BlockSpec((1, rows, tm), lambda b, i: (b, 0, i))

    def cast_spec(w):
        rows = _cast_rows(w.shape[0], bsz * tiles)
        last = w.shape[0] // rows - 1
        return pl.BlockSpec((rows, w.shape[1]), lambda b, i: (jnp.minimum(b * tiles + i, last), 0))

    cast_specs = [cast_spec(w) for w in later_weights]
    outs = pl.pallas_call(
        functools.partial(_inproj_kernel, tm=tm, hw=hw, aw=aw, layer=layer),
        out_shape=[bf(hw), bf(hw), jax.ShapeDtypeStruct((bsz, t, hw), F32), bf(hw), bf(hw),
                   bf(aw), bf(aw), jax.ShapeDtypeStruct((bsz, aw, t), BF16),
                   jax.ShapeDtypeStruct((bsz, ATT_HEADS * nb, t), F32)]
                  + [jax.ShapeDtypeStruct(w.shape, BF16) for w in later_weights],
        grid=(bsz, tiles),
        in_specs=[tok(d),
                  pl.BlockSpec((1, 6, d), lambda b, i: (b, 0, 0)),
                  const((1, d)), const(w_in.shape), const(lb_logits.shape),
                  const((1, aw)), const((1, aw)), const((MXU_TILE, MXU_TILE))] + cast_specs,
        out_specs=[tok(hw), tok(hw), tok(hw), tok(hw), tok(hw), tok(aw), tok(aw), tok_t(aw),
                   tok_t(ATT_HEADS * nb)] + cast_specs,
        scratch_shapes=[pltpu.VMEM((nb, aw), F32)],
        compiler_params=pltpu.CompilerParams(
            dimension_semantics=("arbitrary", "arbitrary"), vmem_limit_bytes=VMEM_LIMIT),
        name="inproj",
    )(x, mod, g1.reshape(1, d), w_in.astype(BF16), lb_logits,
      jnp.tile(qg, ATT_HEADS).reshape(1, aw), jnp.tile(kg, ATT_HEADS).reshape(1, aw), pool, *later_weights)
    return outs[:9], outs[9:]


HG_LEVELS = (128, 64, 32, 16, 8)
HG_DIAG = 8
HG_UNROLL = 4
BF16_ROWS = 16


def _group_rows(b, first, step, rows):
    n = b.shape[0] // rows
    return jnp.concatenate(
        [jnp.broadcast_to(b[first + g * step:first + g * step + 1], (rows, b.shape[1])) for g in range(n)], axis=0)


def _blend_rows(q, kk, m):
    n = q.shape[0] // m
    return jnp.concatenate([(q if g % 2 else kk)[g * m:(g + 1) * m] for g in range(n)], axis=0)


def _hgrn_kernel(q_ref, k_ref, lf_ref, v_ref, g_ref, gain_ref, o_ref, st_ref):
    c_len = HG_CHUNK
    t_len, hw = q_ref.shape[1], q_ref.shape[2]
    d = hw // HG_HEADS
    r = lax.broadcasted_iota(jnp.int32, (c_len, c_len), 0)
    c = lax.broadcasted_iota(jnp.int32, (c_len, c_len), 1)
    tri = (r >= c).astype(BF16)
    level_masks = [(r // (2 * m) == c // (2 * m)) & ((r // m) % 2 == 1) & ((c // m) % 2 == 0) for m in HG_LEVELS]
    diag_mask = (r // HG_DIAG == c // HG_DIAG) & (r >= c)
    gain = gain_ref[...]

    st_ref[...] = jnp.zeros_like(st_ref)

    def body(ci, carry):
        sl = pl.ds(pl.multiple_of(ci * c_len, c_len), c_len)
        q = q_ref[0, sl, :]
        kk = k_ref[0, sl, :]
        lf = lf_ref[0, sl, :]
        v = v_ref[0, sl, :]

        l1 = lf.astype(BF16)
        l2 = (lf - l1.astype(F32)).astype(BF16)
        b = _dot(tri, l2) + _dot(tri, l1)
        b_last = b[c_len - 1:c_len]

        q_in = q * jnp.exp2(b).astype(BF16)
        k_st = kk * jnp.exp2(b_last - b).astype(BF16)
        decay = jnp.exp2(b_last)
        pairs = []
        for m in HG_LEVELS:
            w = jnp.exp2(-jnp.abs(b - _group_rows(b, m - 1, 2 * m, 2 * m)))
            if m % BF16_ROWS == 0:
                x = _blend_rows(q, kk, m) * w.astype(BF16)
            else:
                x = (_blend_rows(q.astype(F32), kk.astype(F32), m) * w).astype(BF16)
            pairs.append((x, x))
        dd = b - _group_rows(b, 0, HG_DIAG, HG_DIAG)
        pairs.append((q * jnp.exp2(dd).astype(BF16), kk * jnp.exp2(-dd).astype(BF16)))

        outs = []
        for h in range(HG_HEADS):
            hs = slice(h * d, (h + 1) * d)
            st = st_ref[h]
            o = _dot_nt(q_in[:, hs], st.astype(BF16))
            st_ref[h] = decay[:, hs] * st + _dot_tn(v[:, hs], k_st[:, hs])
            attn = jnp.zeros((c_len, c_len), F32)
            for (qw, kw), mask in zip(pairs, level_masks + [diag_mask]):
                attn = jnp.where(mask, _dot_nt(qw[:, hs], kw[:, hs]), attn)
            o = o + _dot(attn.astype(BF16), v[:, hs])
            outs.append(o * lax.rsqrt(jnp.mean(o * o, axis=-1, keepdims=True) + EPS) * gain)
        o_ref[0, sl, :] = (jnp.concatenate(outs, axis=1) * g_ref[0, sl, :].astype(F32)).astype(o_ref.dtype)
        return carry

    lax.fori_loop(0, t_len // c_len, body, 0, unroll=HG_UNROLL)


def _hgrn(hq, kk, lf, hv, hg, gain):
    bsz, t, hw = hq.shape
    d = hw // HG_HEADS
    blk = pl.BlockSpec((1, t, hw), lambda b: (b, 0, 0))
    return pl.pallas_call(
        _hgrn_kernel,
        out_shape=jax.ShapeDtypeStruct((bsz, t, hw), BF16),
        grid=(bsz,),
        in_specs=[blk, blk, blk, blk, blk, pl.BlockSpec((1, d), lambda b: (0, 0))],
        out_specs=blk,
        scratch_shapes=[pltpu.VMEM((HG_HEADS, d, d), F32)],
        compiler_params=pltpu.CompilerParams(dimension_semantics=("parallel",), vmem_limit_bytes=VMEM_LIMIT),
        name="hgrn",
    )(hq, kk, lf, hv, hg, gain.reshape(1, d))


SLOPE_PARTS = 3
AUG_ROWS = 8
QUERY_TILE = 2 * MXU_TILE


def _moba_kernel(slope_ref, q_ref, k_ref, vt_ref, gate_ref, o_ref, kaug_scr, qaug_scr, vaug_scr):
    p = pl.program_id(1)
    t_len, width = k_ref.shape[1], k_ref.shape[2]
    nb = t_len // BLOCK
    dh = width // 2

    k = k_ref[0]
    vt = vt_ref[0]
    qt = q_ref[0].astype(F32).T
    row = lax.broadcasted_iota(jnp.int32, k.shape, 0)
    ln = lax.broadcasted_iota(jnp.int32, k.shape, 1)
    in_block = (row % BLOCK).astype(F32)
    block_start = (row - row % BLOCK).astype(F32)
    feat = lax.broadcasted_iota(jnp.int32, (width, t_len), 0)
    jrow = lax.broadcasted_iota(jnp.int32, (nb, t_len), 0)
    qblk = lax.broadcasted_iota(jnp.int32, (nb, t_len), 1) // BLOCK
    crow = lax.broadcasted_iota(jnp.int32, (AUG_ROWS, t_len), 0)
    for a in range(2):
        x0 = (1 - a) * dh
        y0 = x0 + AUG_ROWS
        extra = jnp.where(ln < x0 + SLOPE_PARTS, in_block,
                          jnp.where(ln < x0 + 2 * SLOPE_PARTS, block_start,
                                    (ln - y0 == row // BLOCK).astype(F32)))
        kaug_scr[a] = jnp.where((ln >= x0) & (ln < y0 + nb), extra.astype(BF16), k)
        vaug_scr[a] = jnp.concatenate([vt[a * dh:(a + 1) * dh], jnp.ones((BF16_ROWS, t_len), BF16)], axis=0)

        g = jnp.where(jrow < qblk, gate_ref[0, a * nb:(a + 1) * nb, :], -jnp.inf)
        rank = jnp.zeros((nb, t_len), jnp.int32)
        for i in range(nb - 1):
            gi = g[i:i + 1, :]
            rank = rank + ((gi > g) | ((gi == g) & (i < jrow))).astype(jnp.int32)
        sel_t = jnp.where((rank < TOPK) | (jrow >= qblk), 0.0, NEG)

        slope_t = jnp.zeros((AUG_ROWS, t_len), F32)
        for i in range(SLOPE_PARTS):
            slope_t = jnp.where((crow == i) | (crow == SLOPE_PARTS + i), slope_ref[i, 2 * p + a], slope_t)
        pieces = [slope_t, sel_t, jnp.zeros((width - x0 - AUG_ROWS - nb, t_len), F32)]
        if x0:
            pieces.insert(0, jnp.zeros((x0, t_len), F32))
        own = (feat >= a * dh) & (feat < (a + 1) * dh)
        qaug_scr[a] = jnp.where(own, qt, jnp.concatenate(pieces, axis=0)).astype(BF16)

    key = lax.broadcasted_iota(jnp.int32, (QUERY_TILE, QUERY_TILE), 0)
    qry = lax.broadcasted_iota(jnp.int32, (QUERY_TILE, QUERY_TILE), 1)
    causal = key <= qry

    def head(lo, a):
        hi = lo + QUERY_TILE
        s = _dot(kaug_scr[a, 0:hi, :], qaug_scr[a, :, lo:hi])
        s_own = jnp.where(causal, s[lo:], NEG)
        s = s_own if lo == 0 else jnp.concatenate([s[:lo], s_own], axis=0)
        m = jnp.max(s, axis=0, keepdims=True)
        acc = _dot(vaug_scr[a, :, 0:hi], jnp.exp2(s - m).astype(BF16))
        return acc[0:dh] / acc[dh:dh + 1]

    for lo in range(0, t_len, QUERY_TILE):
        out_t = jnp.concatenate([head(lo, 0), head(lo, 1)], axis=0)
        o_ref[0, lo:lo + QUERY_TILE, :] = out_t.T.astype(o_ref.dtype)


def _moba(aq, ak, av_t, gate_t):
    bsz, t, aw = aq.shape
    dh = aw // ATT_HEADS
    nb = t // BLOCK
    assert nb % 8 == 0 and AUG_ROWS + nb <= dh and 2 * SLOPE_PARTS <= AUG_ROWS
    slopes = LOG2E * jnp.exp2(-8.0 * jnp.arange(1, ATT_HEADS + 1, dtype=F32) / ATT_HEADS)
    parts = []
    for _ in range(SLOPE_PARTS):
        parts.append(slopes.astype(BF16).astype(F32))
        slopes = slopes - parts[-1]
    pair_spec = pl.BlockSpec((1, t, 2 * dh), lambda b, p: (b, 0, p))
    return pl.pallas_call(
        _moba_kernel,
        out_shape=jax.ShapeDtypeStruct((bsz, t, aw), BF16),
        grid=(bsz, ATT_HEADS // 2),
        in_specs=[pl.BlockSpec(memory_space=pltpu.SMEM),
                  pair_spec, pair_spec,
                  pl.BlockSpec((1, 2 * dh, t), lambda b, p: (b, p, 0)),
                  pl.BlockSpec((1, 2 * nb, t), lambda b, p: (b, p, 0))],
        out_specs=pair_spec,
        scratch_shapes=[pltpu.VMEM((2, t, 2 * dh), BF16), pltpu.VMEM((2, 2 * dh, t), BF16),
                        pltpu.VMEM((2, dh + BF16_ROWS, t), BF16)],
        compiler_params=pltpu.CompilerParams(
            dimension_semantics=("parallel", "parallel"), vmem_limit_bytes=VMEM_LIMIT),
        name="moba",
    )(jnp.stack(parts), aq, ak, av_t, gate_t)


def _ffn_kernel(x_ref, mh_ref, ma_ref, mod_ref, g2_ref, wo_ref, wg_ref, wu_ref, wd_ref, o_ref, *, ff_chunks):
    hw = mh_ref.shape[2]
    mod = mod_ref[0]
    mixed = _dot(mh_ref[0], wo_ref[0:hw]) + _dot(ma_ref[0], wo_ref[hw:])
    x1 = x_ref[0] + mod[2:3] * mixed
    xn = x1 * lax.rsqrt(jnp.mean(x1 * x1, axis=-1, keepdims=True) + EPS)
    hb = ((xn * g2_ref[...]) * (1.0 + mod[4:5]) + mod[3:4]).astype(BF16)
    tiles = wg_ref.shape[1] // MXU_TILE
    bounds = [-(-tiles * ci // ff_chunks) * MXU_TILE for ci in range(ff_chunks)] + [wg_ref.shape[1]]
    y = jnp.zeros_like(x1)
    for lo, hi in zip(bounds[:-1], bounds[1:]):
        act = _silu(_dot(hb, wg_ref[:, lo:hi])) * _dot(hb, wu_ref[:, lo:hi])
        y = y + _dot(act.astype(BF16), wd_ref[lo:hi])
    o_ref[0] = x1 + mod[5:6] * y


def _ffn(x, mh, ma, mod, g2, w_out, w_gate, w_up, w_down, *, tm=512, ff_chunks=2):
    bsz, t, d = x.shape
    tok = lambda w: pl.BlockSpec((1, tm, w), lambda b, i: (b, i, 0))
    const = lambda shape: pl.BlockSpec(shape, lambda b, i: (0,) * len(shape), pipeline_mode=pl.Buffered(1))
    return pl.pallas_call(
        functools.partial(_ffn_kernel, ff_chunks=ff_chunks),
        out_shape=jax.ShapeDtypeStruct((bsz, t, d), x.dtype),
        grid=(bsz, t // tm),
        in_specs=[tok(d), tok(mh.shape[2]), tok(ma.shape[2]),
                  pl.BlockSpec((1, 6, d), lambda b, i: (b, 0, 0)),
                  const((1, d)), const(w_out.shape), const(w_gate.shape), const(w_up.shape), const(w_down.shape)],
        out_specs=tok(d),
        compiler_params=pltpu.CompilerParams(
            dimension_semantics=("parallel", "parallel"), vmem_limit_bytes=VMEM_LIMIT),
        name="ffn",
    )(x, mh, ma, mod, g2.reshape(1, d), w_out, w_gate, w_up, w_down)


def kernel(x, c, w_ada, b_ada, norm1_g, w_in, lb_logits, hg_norm_g, q_norm_g, k_norm_g,
           w_out, norm2_g, w_gate, w_up, w_down):
    bsz, _, d = x.shape
    for l in range(w_ada.shape[0]):
        mod = _adaln(c, w_ada[l], b_ada[l]).reshape(bsz, 6, d)
        (hq, kk, lf, hv, hg, aq, ak, av_t, gate_t), ffn_weights = _inproj(
            x, mod, norm1_g[l], w_in[l], lb_logits, q_norm_g[l], k_norm_g[l],
            (w_out[l], w_gate[l], w_up[l], w_down[l]), layer=l)
        o_hg = _hgrn(hq, kk, lf, hv, hg, hg_norm_g[l])
        o_att = _moba(aq, ak, av_t, gate_t)
        x = _ffn(x, o_hg, o_att, mod, norm2_g[l], *ffn_weights)
    return x
```

```python
import functools

import jax
import jax.numpy as jnp
from jax import lax
from jax.experimental import pallas as pl
from jax.experimental.pallas import tpu as pltpu

F32 = jnp.float32
BF16 = jnp.bfloat16

HG_HEADS = 4
ATT_HEADS = 8
BLOCK = 256
TOPK = 3
HG_CHUNK = 256
EPS = 1e-6
NEG = -(2.0 ** 100)

LOG2E = 1.4426950408889634
MXU_TILE = 256

VMEM_LIMIT = 56 * 1024 * 1024


def _silu(t):
    return t * jax.nn.sigmoid(t)


def _dot(a, b):
    return jnp.dot(a, b, preferred_element_type=F32)


def _dot_nt(a, b):
    return lax.dot_general(a, b, (((1,), (1,)), ((), ())), preferred_element_type=F32)


def _dot_tn(a, b):
    return lax.dot_general(a, b, (((0,), (0,)), ((), ())), preferred_element_type=F32)


def _split2(t):
    hi = t.astype(BF16)
    lo = (t - hi.astype(F32)).astype(BF16)
    return hi, lo


def _dot3(a, b, dot=_dot):
    a_hi, a_lo = _split2(a)
    b_hi, b_lo = _split2(b)
    return (dot(a_hi, b_lo) + dot(a_lo, b_hi)) + dot(a_hi, b_hi)


def _adaln_kernel(c_ref, w_ref, b_ref, o_ref):
    o_ref[...] = _dot3(_silu(c_ref[...]), w_ref[...]) + b_ref[...]


def _adaln(c, w, b):
    bsz, d = c.shape
    n = w.shape[1]
    tn = 1024
    return pl.pallas_call(
        _adaln_kernel,
        out_shape=jax.ShapeDtypeStruct((bsz, n), F32),
        grid=(n // tn,),
        in_specs=[pl.BlockSpec((bsz, d), lambda j: (0, 0)),
                  pl.BlockSpec((d, tn), lambda j: (0, j)),
                  pl.BlockSpec((1, tn), lambda j: (0, j))],
        out_specs=pl.BlockSpec((bsz, tn), lambda j: (0, j)),
        name="adaln",
    )(c, w, b.reshape(1, n))


def _inproj_kernel(x_ref, mod_ref, g1_ref, w_ref, lbl_ref, qg_ref, kg_ref, pool_ref,
                   f32_0, f32_1, f32_2, f32_3,
                   hq_ref, kk_ref, lf_ref, hv_ref, hg_ref, aq_ref, ak_ref, avt_ref, gate_ref,
                   bf16_0, bf16_1, bf16_2, bf16_3,
                   kmean_scr, *, tm, hw, aw, layer):
    i = pl.program_id(1)

    for src, dst in ((f32_0, bf16_0), (f32_1, bf16_1), (f32_2, bf16_2), (f32_3, bf16_3)):
        dst[...] = src[...].astype(BF16)

    dh = aw // ATT_HEADS
    nb = kmean_scr.shape[0]

    x = x_ref[0]
    xn = x * lax.rsqrt(jnp.mean(x * x, axis=-1, keepdims=True) + EPS)
    mod = mod_ref[0]
    h = (xn * g1_ref[...]) * (1.0 + mod[1:2]) + mod[0:1]
    hb = h.astype(BF16)

    def proj(lo, width):
        return _dot(hb, w_ref[:, lo:lo + width])

    def head_norm(t, gain):
        t2 = (t * t).astype(BF16)
        pw = pool_ref.shape[0]
        ms = jnp.concatenate([_dot(t2[:, c:c + pw], pool_ref[...]) for c in range(0, aw, pw)], axis=1)
        return t * lax.rsqrt(ms + EPS) * gain

    p_k = proj(4 * hw + aw, aw)
    p_q = proj(4 * hw, aw)
    kn = head_norm(p_k, kg_ref[...])
    ak_ref[0] = kn.astype(BF16)

    @pl.when(i == 0)
    def _():
        kmean_scr[...] = jnp.zeros_like(kmean_scr)

    per_tile = tm // BLOCK
    for s in range(per_tile):
        kmean_scr[pl.ds(i * per_tile + s, 1), :] = jnp.mean(
            kn[s * BLOCK:(s + 1) * BLOCK], axis=0, keepdims=True)

    p_v = proj(4 * hw + 2 * aw, aw)
    qn = head_norm(p_q, qg_ref[...])
    aq_ref[0] = (qn * (dh ** -0.5 * LOG2E)).astype(BF16)
    kmean = kmean_scr[...]
    rows = lax.broadcasted_iota(jnp.int32, (ATT_HEADS * nb, aw), 0)
    cols = lax.broadcasted_iota(jnp.int32, (ATT_HEADS * nb, aw), 1)
    sel = jnp.where(rows // nb == cols // dh, jnp.concatenate([kmean] * ATT_HEADS, axis=0), 0.0)
    gate_ref[0] = _dot3(sel, qn, dot=_dot_nt)
    p_hq = proj(0, hw)
    avt_ref[0] = p_v.T.astype(BF16)

    p_hf = proj(hw, hw)
    hq_ref[0] = _silu(p_hq).astype(BF16)
    p_hg = proj(3 * hw, hw)
    lbl = lbl_ref[...]
    e = jnp.exp(lbl - jnp.max(lbl, axis=0, keepdims=True))
    lb = jnp.sum(e[0:layer + 1], axis=0, keepdims=True) / jnp.sum(e, axis=0, keepdims=True)
    f = lb + (1.0 - lb) * jax.nn.sigmoid(p_hf)
    lf_ref[0] = jnp.log2(f)
    kk_ref[0] = (1.0 - f).astype(BF16)
    p_hv = proj(2 * hw, hw)
    hg_ref[0] = _silu(p_hg).astype(BF16)
    hv_ref[0] = p_hv.astype(BF16)


def _cast_rows(rows, steps):
    chunk = BF16_ROWS
    while chunk * steps < rows or rows % chunk:
        chunk += BF16_ROWS
    return chunk


def _inproj(x, mod, g1, w_in, lb_logits, qg, kg, later_weights, *, layer, tm=512):
    bsz, t, d = x.shape
    hw = lb_logits.shape[1]
    aw = (w_in.shape[1] - 4 * hw) // 3
    dh = aw // ATT_HEADS
    nb = t // BLOCK
    tiles = t // tm
    lanes = jnp.arange(MXU_TILE)
    pool = jnp.where(lanes[:, None] // dh == lanes[None, :] // dh, 1.0 / dh, 0.0).astype(BF16)
    tok = lambda w: pl.BlockSpec((1, tm, w), lambda b, i: (b, i, 0))
    const = lambda shape: pl.BlockSpec(shape, lambda b, i: (0,) * len(shape), pipeline_mode=pl.Buffered(1))
    bf = lambda w: jax.ShapeDtypeStruct((bsz, t, w), BF16)
    tok_t = lambda rows: pl.BlockSpec((1, rows, tm), lambda b, i: (b, 0, i))

    def cast_spec(w):
        rows = _cast_rows(w.shape[0], bsz * tiles)
        last = w.shape[0] // rows - 1
        return pl.BlockSpec((rows, w.shape[1]), lambda b, i: (jnp.minimum(b * tiles + i, last), 0))

    cast_specs = [cast_spec(w) for w in later_weights]
    outs = pl.pallas_call(
        functools.partial(_inproj_kernel, tm=tm, hw=hw, aw=aw, layer=layer),
        out_shape=[bf(hw), bf(hw), jax.ShapeDtypeStruct((bsz, t, hw), F32), bf(hw), bf(hw),
                   bf(aw), bf(aw), jax.ShapeDtypeStruct((bsz, aw, t), BF16),
                   jax.ShapeDtypeStruct((bsz, ATT_HEADS * nb, t), F32)]
                  + [jax.ShapeDtypeStruct(w.shape, BF16) for w in later_weights],
        grid=(bsz, tiles),
        in_specs=[tok(d),
                  pl.BlockSpec((1, 6, d), lambda b, i: (b, 0, 0)),
                  const((1, d)), const(w_in.shape), const(lb_logits.shape),
                  const((1, aw)), const((1, aw)), const((MXU_TILE, MXU_TILE))] + cast_specs,
        out_specs=[tok(hw), tok(hw), tok(hw), tok(hw), tok(hw), tok(aw), tok(aw), tok_t(aw),
                   tok_t(ATT_HEADS * nb)] + cast_specs,
        scratch_shapes=[pltpu.VMEM((nb, aw), F32)],
        compiler_params=pltpu.CompilerParams(
            dimension_semantics=("arbitrary", "arbitrary"), vmem_limit_bytes=VMEM_LIMIT),
        name="inproj",
    )(x, mod, g1.reshape(1, d), w_in.astype(BF16), lb_logits,
      jnp.tile(qg, ATT_HEADS).reshape(1, aw), jnp.tile(kg, ATT_HEADS).reshape(1, aw), pool, *later_weights)
    return outs[:9], outs[9:]


HG_LEVELS = (128, 64, 32, 16, 8)
HG_DIAG = 8
HG_UNROLL = 8
BF16_ROWS = 16


def _group_rows(b, first, step, rows):
    n = b.shape[0] // rows
    return jnp.concatenate(
        [jnp.broadcast_to(b[first + g * step:first + g * step + 1], (rows, b.shape[1])) for g in range(n)], axis=0)


def _blend_rows(q, kk, m):
    n = q.shape[0] // m
    return jnp.concatenate([(q if g % 2 else kk)[g * m:(g + 1) * m] for g in range(n)], axis=0)


def _hgrn_tables(c_len):
    r = lax.broadcasted_iota(jnp.int32, (c_len, c_len), 0)
    c = lax.broadcasted_iota(jnp.int32, (c_len, c_len), 1)
    tri = (r >= c).astype(BF16)
    masks = [(r // (2 * m) == c // (2 * m)) & ((r // m) % 2 == 1) & ((c // m) % 2 == 0) for m in HG_LEVELS]
    masks.append((r // HG_DIAG == c // HG_DIAG) & (r >= c))
    return tri, masks


def _hgrn_factors(rows, q_ref, k_ref, lf_ref, v_ref, tri):
    c_len = HG_CHUNK
    q = q_ref[0, rows, :]
    kk = k_ref[0, rows, :]
    lf = lf_ref[0, rows, :]
    v = v_ref[0, rows, :]

    l1 = lf.astype(BF16)
    l2 = (lf - l1.astype(F32)).astype(BF16)
    b = _dot(tri, l2) + _dot(tri, l1)
    b_last = b[c_len - 1:c_len]

    q_in = q * jnp.exp2(b).astype(BF16)
    k_st = kk * jnp.exp2(b_last - b).astype(BF16)
    decay = jnp.exp2(b_last)
    pairs = []
    for m in HG_LEVELS:
        w = jnp.exp2(-jnp.abs(b - _group_rows(b, m - 1, 2 * m, 2 * m)))
        if m % BF16_ROWS == 0:
            x = _blend_rows(q, kk, m) * w.astype(BF16)
        else:
            x = (_blend_rows(q.astype(F32), kk.astype(F32), m) * w).astype(BF16)
        pairs.append((x, x))
    dd = b - _group_rows(b, 0, HG_DIAG, HG_DIAG)
    pairs.append((q * jnp.exp2(dd).astype(BF16), kk * jnp.exp2(-dd).astype(BF16)))
    return q_in, k_st, decay, pairs, v


def _hgrn_attend(factors, hs, st, masks):
    q_in, k_st, decay, pairs, v = factors
    o_inter = _dot_nt(q_in[:, hs], st.astype(BF16))
    st_next = decay[:, hs] * st + _dot_tn(v[:, hs], k_st[:, hs])
    attn = jnp.zeros((HG_CHUNK, HG_CHUNK), F32)
    for (qw, kw), mask in zip(pairs, masks):
        attn = jnp.where(mask, _dot_nt(qw[:, hs], kw[:, hs]), attn)
    return o_inter, st_next, attn.astype(BF16)


def _hgrn_kernel(q_ref, k_ref, lf_ref, v_ref, g_ref, gain_ref, o_ref, st_ref):
    tri, masks = _hgrn_tables(HG_CHUNK)
    gain = gain_ref[...]
    d = st_ref.shape[1]
    st_ref[...] = jnp.zeros_like(st_ref)

    def body(ti, carry):
        base = ti * (HG_UNROLL * HG_CHUNK)
        rows = [pl.ds(pl.multiple_of(base + u * HG_CHUNK, HG_CHUNK), HG_CHUNK) for u in range(HG_UNROLL)]
        states = [st_ref[h] for h in range(HG_HEADS)]
        factors = {0: _hgrn_factors(rows[0], q_ref, k_ref, lf_ref, v_ref, tri)}
        pieces = {u: [] for u in range(HG_UNROLL)}

        def finish(u, hs, o_inter, attn):
            o = o_inter + _dot(attn, factors[u][4][:, hs])
            pieces[u].append(o * lax.rsqrt(jnp.mean(o * o, axis=-1, keepdims=True) + EPS) * gain)
            if len(pieces[u]) == HG_HEADS:
                out = jnp.concatenate(pieces[u], axis=1) * g_ref[0, rows[u], :].astype(F32)
                o_ref[0, rows[u], :] = out.astype(o_ref.dtype)

        pending = None
        for u in range(HG_UNROLL):
            if u + 1 < HG_UNROLL:
                factors[u + 1] = _hgrn_factors(rows[u + 1], q_ref, k_ref, lf_ref, v_ref, tri)
            for h in range(HG_HEADS):
                hs = slice(h * d, (h + 1) * d)
                o_inter, states[h], attn = _hgrn_attend(factors[u], hs, states[h], masks)
                if pending is not None:
                    finish(*pending)
                pending = (u, hs, o_inter, attn)
        finish(*pending)
        for h in range(HG_HEADS):
            st_ref[h] = states[h]
        return carry

    lax.fori_loop(0, q_ref.shape[1] // (HG_UNROLL * HG_CHUNK), body, 0)


def _hgrn(hq, kk, lf, hv, hg, gain):
    bsz, t, hw = hq.shape
    d = hw // HG_HEADS
    blk = pl.BlockSpec((1, t, hw), lambda b: (b, 0, 0))
    return pl.pallas_call(
        _hgrn_kernel,
        out_shape=jax.ShapeDtypeStruct((bsz, t, hw), BF16),
        grid=(bsz,),
        in_specs=[blk, blk, blk, blk, blk, pl.BlockSpec((1, d), lambda b: (0, 0))],
        out_specs=blk,
        scratch_shapes=[pltpu.VMEM((HG_HEADS, d, d), F32)],
        compiler_params=pltpu.CompilerParams(dimension_semantics=("parallel",), vmem_limit_bytes=VMEM_LIMIT),
        name="hgrn",
    )(hq, kk, lf, hv, hg, gain.reshape(1, d))


SLOPE_PARTS = 3
AUG_ROWS = 8
QUERY_TILE = 2 * MXU_TILE


def _moba_prepare(slope_ref, pair, q_ref, k_ref, vt_ref, gate_ref, kaug_scr, qaug_scr, vaug_scr):
    t_len, width = k_ref.shape[1], k_ref.shape[2]
    nb = t_len // BLOCK
    dh = width // 2
    k = k_ref[0]
    vt = vt_ref[0]
    qt = q_ref[0].astype(F32).T
    row = lax.broadcasted_iota(jnp.int32, k.shape, 0)
    ln = lax.broadcasted_iota(jnp.int32, k.shape, 1)
    in_block = (row % BLOCK).astype(F32)
    block_start = (row - row % BLOCK).astype(F32)
    feat = lax.broadcasted_iota(jnp.int32, (width, t_len), 0)
    jrow = lax.broadcasted_iota(jnp.int32, (nb, t_len), 0)
    qblk = lax.broadcasted_iota(jnp.int32, (nb, t_len), 1) // BLOCK
    crow = lax.broadcasted_iota(jnp.int32, (AUG_ROWS, t_len), 0)
    for a in range(2):
        x0 = (1 - a) * dh
        y0 = x0 + AUG_ROWS
        extra = jnp.where(ln < x0 + SLOPE_PARTS, in_block,
                          jnp.where(ln < x0 + 2 * SLOPE_PARTS, block_start,
                                    (ln - y0 == row // BLOCK).astype(F32)))
        kaug_scr[a] = jnp.where((ln >= x0) & (ln < y0 + nb), extra.astype(BF16), k)
        vaug_scr[a] = jnp.concatenate([vt[a * dh:(a + 1) * dh], jnp.ones((BF16_ROWS, t_len), BF16)], axis=0)

        g = jnp.where(jrow < qblk, gate_ref[0, a * nb:(a + 1) * nb, :], -jnp.inf)
        rank = jnp.zeros((nb, t_len), jnp.int32)
        for i in range(nb - 1):
            gi = g[i:i + 1, :]
            rank = rank + ((gi > g) | ((gi == g) & (i < jrow))).astype(jnp.int32)
        sel_t = jnp.where((rank < TOPK) | (jrow >= qblk), 0.0, NEG)

        slope_t = jnp.zeros((AUG_ROWS, t_len), F32)
        for i in range(SLOPE_PARTS):
            slope_t = jnp.where((crow == i) | (crow == SLOPE_PARTS + i), slope_ref[i, 2 * pair + a], slope_t)
        pieces = [slope_t, sel_t, jnp.zeros((width - x0 - AUG_ROWS - nb, t_len), F32)]
        if x0:
            pieces.insert(0, jnp.zeros((x0, t_len), F32))
        own = (feat >= a * dh) & (feat < (a + 1) * dh)
        qaug_scr[a] = jnp.where(own, qt, jnp.concatenate(pieces, axis=0)).astype(BF16)


def _moba_scores(lo, a, kaug_scr, qaug_scr):
    hi = lo + QUERY_TILE
    key = lax.broadcasted_iota(jnp.int32, (QUERY_TILE, QUERY_TILE), 0)
    qry = lax.broadcasted_iota(jnp.int32, (QUERY_TILE, QUERY_TILE), 1)
    s = _dot(kaug_scr[a, 0:hi, :], qaug_scr[a, :, lo:hi])
    s_own = jnp.where(key <= qry, s[lo:], NEG)
    return s_own if lo == 0 else jnp.concatenate([s[:lo], s_own], axis=0)


def _moba_probs(s):
    m = jnp.max(s, axis=0, keepdims=True)
    return jnp.exp2(s - m).astype(BF16)


def _moba_values(lo, a, p, vaug_scr):
    dh = vaug_scr.shape[1] - BF16_ROWS
    acc = _dot(vaug_scr[a, :, 0:lo + QUERY_TILE], p)
    return acc[0:dh] / acc[dh:dh + 1]


def _moba_kernel(slope_ref, q_ref, k_ref, vt_ref, gate_ref, o_ref, kaug_scr, qaug_scr, vaug_scr):
    _moba_prepare(slope_ref, pl.program_id(1), q_ref, k_ref, vt_ref, gate_ref, kaug_scr, qaug_scr, vaug_scr)
    chains = [(lo, a) for lo in range(0, k_ref.shape[1], QUERY_TILE) for a in range(2)]
    scores, probs, outs = {}, {}, {}
    for i in range(len(chains) + 2):
        if i < len(chains):
            scores[i] = _moba_scores(*chains[i], kaug_scr, qaug_scr)
        if 1 <= i <= len(chains):
            probs[i - 1] = _moba_probs(scores.pop(i - 1))
        if i >= 2:
            lo, a = chains[i - 2]
            outs[a] = _moba_values(lo, a, probs.pop(i - 2), vaug_scr)
            if a == 1:
                o_ref[0, lo:lo + QUERY_TILE, :] = jnp.concatenate([outs[0], outs[1]], axis=0).T.astype(o_ref.dtype)


def _moba(aq, ak, av_t, gate_t):
    bsz, t, aw = aq.shape
    dh = aw // ATT_HEADS
    nb = t // BLOCK
    assert nb % 8 == 0 and AUG_ROWS + nb <= dh and 2 * SLOPE_PARTS <= AUG_ROWS and t % QUERY_TILE == 0
    slopes = LOG2E * jnp.exp2(-8.0 * jnp.arange(1, ATT_HEADS + 1, dtype=F32) / ATT_HEADS)
    parts = []
    for _ in range(SLOPE_PARTS):
        parts.append(slopes.astype(BF16).astype(F32))
        slopes = slopes - parts[-1]
    pair_spec = pl.BlockSpec((1, t, 2 * dh), lambda b, p: (b, 0, p))
    return pl.pallas_call(
        _moba_kernel,
        out_shape=jax.ShapeDtypeStruct((bsz, t, aw), BF16),
        grid=(bsz, ATT_HEADS // 2),
        in_specs=[pl.BlockSpec(memory_space=pltpu.SMEM),
                  pair_spec, pair_spec,
                  pl.BlockSpec((1, 2 * dh, t), lambda b, p: (b, p, 0)),
                  pl.BlockSpec((1, 2 * nb, t), lambda b, p: (b, p, 0))],
        out_specs=pair_spec,
        scratch_shapes=[pltpu.VMEM((2, t, 2 * dh), BF16), pltpu.VMEM((2, 2 * dh, t), BF16),
                        pltpu.VMEM((2, dh + BF16_ROWS, t), BF16)],
        compiler_params=pltpu.CompilerParams(
            dimension_semantics=("parallel", "parallel"), vmem_limit_bytes=VMEM_LIMIT),
        name="moba",
    )(jnp.stack(parts), aq, ak, av_t, gate_t)


def _ffn_kernel(x_ref, mh_ref, ma_ref, mod_ref, g2_ref, wo_ref, wg_ref, wu_ref, wd_ref, o_ref, *, ff_chunks):
    hw = mh_ref.shape[2]
    mod = mod_ref[0]
    tiles = wg_ref.shape[1] // MXU_TILE
    bounds = [-(-tiles * ci // ff_chunks) * MXU_TILE for ci in range(ff_chunks)] + [wg_ref.shape[1]]
    mixed = _dot(mh_ref[0], wo_ref[0:hw]) + _dot(ma_ref[0], wo_ref[hw:])
    x1 = x_ref[0] + mod[2:3] * mixed
    xn = x1 * lax.rsqrt(jnp.mean(x1 * x1, axis=-1, keepdims=True) + EPS)
    hb = ((xn * g2_ref[...]) * (1.0 + mod[4:5]) + mod[3:4]).astype(BF16)
    spans = list(zip(bounds[:-1], bounds[1:]))
    up = lambda lo, hi: (_dot(hb, wg_ref[:, lo:hi]), _dot(hb, wu_ref[:, lo:hi]))
    nxt = up(*spans[0])
    y = jnp.zeros_like(x1)
    for ci, (lo, hi) in enumerate(spans):
        gate, lin = nxt
        if ci + 1 < len(spans):
            nxt = up(*spans[ci + 1])
        y = y + _dot((_silu(gate) * lin).astype(BF16), wd_ref[lo:hi])
    o_ref[0] = x1 + mod[5:6] * y


def _ffn(x, mh, ma, mod, g2, w_out, w_gate, w_up, w_down, *, tm=512, ff_chunks=2):
    bsz, t, d = x.shape
    tok = lambda w: pl.BlockSpec((1, tm, w), lambda b, i: (b, i, 0))
    const = lambda shape: pl.BlockSpec(shape, lambda b, i: (0,) * len(shape), pipeline_mode=pl.Buffered(1))
    return pl.pallas_call(
        functools.partial(_ffn_kernel, ff_chunks=ff_chunks),
        out_shape=jax.ShapeDtypeStruct((bsz, t, d), x.dtype),
        grid=(bsz, t // tm),
        in_specs=[tok(d), tok(mh.shape[2]), tok(ma.shape[2]),
                  pl.BlockSpec((1, 6, d), lambda b, i: (b, 0, 0)),
                  const((1, d)), const(w_out.shape), const(w_gate.shape), const(w_up.shape), const(w_down.shape)],
        out_specs=tok(d),
        compiler_params=pltpu.CompilerParams(
            dimension_semantics=("parallel", "parallel"), vmem_limit_bytes=VMEM_LIMIT),
        name="ffn",
    )(x, mh, ma, mod, g2.reshape(1, d), w_out, w_gate, w_up, w_down)


def kernel(x, c, w_ada, b_ada, norm1_g, w_in, lb_logits, hg_norm_g, q_norm_g, k_norm_g,
           w_out, norm2_g, w_gate, w_up, w_down):
    bsz, _, d = x.shape
    for l in range(w_ada.shape[0]):
        mod = _adaln(c, w_ada[l], b_ada[l]).reshape(bsz, 6, d)
        (hq, kk, lf, hv, hg, aq, ak, av_t, gate_t), ffn_weights = _inproj(
            x, mod, norm1_g[l], w_in[l], lb_logits, q_norm_g[l], k_norm_g[l],
            (w_out[l], w_gate[l], w_up[l], w_down[l]), layer=l)
        o_hg = _hgrn(hq, kk, lf, hv, hg, hg_norm_g[l])
        o_att = _moba(aq, ak, av_t, gate_t)
        x = _ffn(x, o_hg, o_att, mod, norm2_g[l], *ffn_weights)
    return x
```

```python
import functools

import jax
import jax.numpy as jnp
from jax import lax
from jax.experimental import pallas as pl
from jax.experimental.pallas import tpu as pltpu

F32 = jnp.float32
BF16 = jnp.bfloat16

HG_HEADS = 4
ATT_HEADS = 8
BLOCK = 256
TOPK = 3
HG_CHUNK = 256
EPS = 1e-6
NEG = -(2.0 ** 100)

LOG2E = 1.4426950408889634
MXU_TILE = 256

VMEM_LIMIT = 56 * 1024 * 1024


def _silu(t):
    return t * jax.nn.sigmoid(t)


def _dot(a, b):
    return jnp.dot(a, b, preferred_element_type=F32)


def _dot_nt(a, b):
    return lax.dot_general(a, b, (((1,), (1,)), ((), ())), preferred_element_type=F32)


def _dot_tn(a, b):
    return lax.dot_general(a, b, (((0,), (0,)), ((), ())), preferred_element_type=F32)


def _split2(t):
    hi = t.astype(BF16)
    lo = (t - hi.astype(F32)).astype(BF16)
    return hi, lo


def _dot3(a, b, dot=_dot):
    a_hi, a_lo = _split2(a)
    b_hi, b_lo = _split2(b)
    return (dot(a_hi, b_lo) + dot(a_lo, b_hi)) + dot(a_hi, b_hi)


def _adaln_kernel(c_ref, w_ref, b_ref, o_ref):
    o_ref[...] = _dot3(_silu(c_ref[...]), w_ref[...]) + b_ref[...]


def _adaln(c, w, b):
    bsz, d = c.shape
    n = w.shape[1]
    tn = 1024
    return pl.pallas_call(
        _adaln_kernel,
        out_shape=jax.ShapeDtypeStruct((bsz, n), F32),
        grid=(n // tn,),
        in_specs=[pl.BlockSpec((bsz, d), lambda j: (0, 0)),
                  pl.BlockSpec((d, tn), lambda j: (0, j)),
                  pl.BlockSpec((1, tn), lambda j: (0, j))],
        out_specs=pl.BlockSpec((bsz, tn), lambda j: (0, j)),
        name="adaln",
    )(c, w, b.reshape(1, n))


def _inproj_kernel(x_ref, mod_ref, g1_ref, w_ref, lbl_ref, qg_ref, kg_ref, pool_ref,
                   f32_0, f32_1, f32_2, f32_3,
                   hq_ref, kk_ref, lf_ref, hv_ref, hg_ref, aq_ref, ak_ref, avt_ref, gate_ref,
                   bf16_0, bf16_1, bf16_2, bf16_3,
                   kmean_scr, *, tm, hw, aw, layer):
    i = pl.program_id(1)

    @pl.when(i == 0)
    def _():
        kmean_scr[...] = jnp.zeros_like(kmean_scr)

    for src, dst in ((f32_0, bf16_0), (f32_1, bf16_1), (f32_2, bf16_2), (f32_3, bf16_3)):
        dst[...] = src[...].astype(BF16)

    dh = aw // ATT_HEADS
    nb = kmean_scr.shape[0]

    x = x_ref[0]
    xn = x * lax.rsqrt(jnp.mean(x * x, axis=-1, keepdims=True) + EPS)
    mod = mod_ref[0]
    h = (xn * g1_ref[...]) * (1.0 + mod[1:2]) + mod[0:1]
    hb = h.astype(BF16)

    def proj(lo, width):
        return _dot(hb, w_ref[:, lo:lo + width])

    def head_norm(t, gain):
        t2 = (t * t).astype(BF16)
        pw = pool_ref.shape[0]
        ms = jnp.concatenate([_dot(t2[:, c:c + pw], pool_ref[...]) for c in range(0, aw, pw)], axis=1)
        return t * lax.rsqrt(ms + EPS) * gain

    p_k = proj(4 * hw + aw, aw)
    p_q = proj(4 * hw, aw)
    p_v = proj(4 * hw + 2 * aw, aw)
    kn = head_norm(p_k, kg_ref[...])
    ak_ref[0] = kn.astype(BF16)
    per_tile = tm // BLOCK
    for s in range(per_tile):
        kmean_scr[pl.ds(i * per_tile + s, 1), :] = jnp.mean(
            kn[s * BLOCK:(s + 1) * BLOCK], axis=0, keepdims=True)

    p_hq = proj(0, hw)
    qn = head_norm(p_q, qg_ref[...])
    aq_ref[0] = (qn * (dh ** -0.5 * LOG2E)).astype(BF16)

    p_hf = proj(hw, hw)
    kmean = kmean_scr[...]
    rows = lax.broadcasted_iota(jnp.int32, (ATT_HEADS * nb, aw), 0)
    cols = lax.broadcasted_iota(jnp.int32, (ATT_HEADS * nb, aw), 1)
    sel = jnp.where(rows // nb == cols // dh, jnp.concatenate([kmean] * ATT_HEADS, axis=0), 0.0)
    sel_hi, sel_lo = _split2(sel)
    q_hi, q_lo = _split2(qn)
    both = _dot_nt(jnp.concatenate([sel_hi, sel_lo], axis=0), q_hi)
    n_sel = sel.shape[0]
    gate_ref[0] = (_dot_nt(sel_hi, q_lo) + both[n_sel:]) + both[:n_sel]
    avt_ref[0] = p_v.T.astype(BF16)

    p_hg = proj(3 * hw, hw)
    hq_ref[0] = _silu(p_hq).astype(BF16)
    lbl = lbl_ref[...]
    e = jnp.exp(lbl - jnp.max(lbl, axis=0, keepdims=True))
    lb = jnp.sum(e[0:layer + 1], axis=0, keepdims=True) / jnp.sum(e, axis=0, keepdims=True)
    p_hv = proj(2 * hw, hw)
    f = lb + (1.0 - lb) * jax.nn.sigmoid(p_hf)
    lf_ref[0] = jnp.log2(f)
    kk_ref[0] = (1.0 - f).astype(BF16)
    hg_ref[0] = _silu(p_hg).astype(BF16)
    hv_ref[0] = p_hv.astype(BF16)


def _cast_rows(rows, steps):
    chunk = BF16_ROWS
    while chunk * steps < rows or rows % chunk:
        chunk += BF16_ROWS
    return chunk


def _inproj(x, mod, g1, w_in, lb_logits, qg, kg, later_weights, *, layer, tm=512):
    bsz, t, d = x.shape
    hw = lb_logits.shape[1]
    aw = (w_in.shape[1] - 4 * hw) // 3
    dh = aw // ATT_HEADS
    nb = t // BLOCK
    tiles = t // tm
    lanes = jnp.arange(MXU_TILE)
    pool = jnp.where(lanes[:, None] // dh == lanes[None, :] // dh, 1.0 / dh, 0.0).astype(BF16)
    tok = lambda w: pl.BlockSpec((1, tm, w), lambda b, i: (b, i, 0))
    const = lambda shape: pl.BlockSpec(shape, lambda b, i: (0,) * len(shape), pipeline_mode=pl.Buffered(1))
    bf = lambda w: jax.ShapeDtypeStruct((bsz, t, w), BF16)
    tok_t = lambda rows: pl.BlockSpec((1, rows, tm), lambda b, i: (b, 0, i))

    def cast_spec(w):
        rows = _cast_rows(w.shape[0], bsz * tiles)
        last = w.shape[0] // rows - 1
        return pl.BlockSpec((rows, w.shape[1]), lambda b, i: (jnp.minimum(b * tiles + i, last), 0))

    cast_specs = [cast_spec(w) for w in later_weights]
    outs = pl.pallas_call(
        functools.partial(_inproj_kernel, tm=tm, hw=hw, aw=aw, layer=layer),
        out_shape=[bf(hw), bf(hw), jax.ShapeDtypeStruct((bsz, t, hw), F32), bf(hw), bf(hw),
                   bf(aw), bf(aw), jax.ShapeDtypeStruct((bsz, aw, t), BF16),
                   jax.ShapeDtypeStruct((bsz, ATT_HEADS * nb, t), F32)]
                  + [jax.ShapeDtypeStruct(w.shape, BF16) for w in later_weights],
        grid=(bsz, tiles),
        in_specs=[tok(d),
                  pl.BlockSpec((1, 6, d), lambda b, i: (b, 0, 0)),
                  const((1, d)), const(w_in.shape), const(lb_logits.shape),
                  const((1, aw)), const((1, aw)), const((MXU_TILE, MXU_TILE))] + cast_specs,
        out_specs=[tok(hw), tok(hw), tok(hw), tok(hw), tok(hw), tok(aw), tok(aw), tok_t(aw),
                   tok_t(ATT_HEADS * nb)] + cast_specs,
        scratch_shapes=[pltpu.VMEM((nb, aw), F32)],
        compiler_params=pltpu.CompilerParams(
            dimension_semantics=("arbitrary", "arbitrary"), vmem_limit_bytes=VMEM_LIMIT),
        name="inproj",
    )(x, mod, g1.reshape(1, d), w_in.astype(BF16), lb_logits,
      jnp.tile(qg, ATT_HEADS).reshape(1, aw), jnp.tile(kg, ATT_HEADS).reshape(1, aw), pool, *later_weights)
    return outs[:9], outs[9:]


HG_LEVELS = (128, 64, 32, 16, 8)
HG_DIAG = 8
HG_EXP_BOUND = 100.0
HG_UNROLL = 8
BF16_ROWS = 16


def _group_rows(b, first, step, rows):
    n = b.shape[0] // rows
    return jnp.concatenate(
        [jnp.broadcast_to(b[first + g * step:first + g * step + 1], (rows, b.shape[1])) for g in range(n)], axis=0)


def _blend_rows(q, kk, m):
    n = q.shape[0] // m
    return jnp.concatenate([(q if g % 2 else kk)[g * m:(g + 1) * m] for g in range(n)], axis=0)


def _hgrn_tables(c_len):
    r = lax.broadcasted_iota(jnp.int32, (c_len, c_len), 0)
    c = lax.broadcasted_iota(jnp.int32, (c_len, c_len), 1)
    tri = (r >= c).astype(BF16)
    masks = [(r // (2 * m) == c // (2 * m)) & ((r // m) % 2 == 1) & ((c // m) % 2 == 0) for m in HG_LEVELS]
    masks.append((r // HG_DIAG == c // HG_DIAG) & (r >= c))
    return tri, masks


def _hgrn_factors(rows, q_ref, k_ref, lf_ref, v_ref, tri):
    c_len = HG_CHUNK
    q = q_ref[0, rows, :]
    kk = k_ref[0, rows, :]
    lf = lf_ref[0, rows, :]
    v = v_ref[0, rows, :]

    l1 = lf.astype(BF16)
    l2 = (lf - l1.astype(F32)).astype(BF16)
    b = _dot(tri, l2) + _dot(tri, l1)
    b_last = b[c_len - 1:c_len]

    q_in = q * jnp.exp2(b).astype(BF16)
    k_st = kk * jnp.exp2(b_last - b).astype(BF16)
    decay = jnp.exp2(b_last)
    pairs = []
    for m in HG_LEVELS:
        w = jnp.exp2(-jnp.abs(b - _group_rows(b, m - 1, 2 * m, 2 * m)))
        if m % BF16_ROWS == 0:
            x = _blend_rows(q, kk, m) * w.astype(BF16)
        else:
            x = (_blend_rows(q.astype(F32), kk.astype(F32), m) * w).astype(BF16)
        pairs.append((x, x))
    dd = b - _group_rows(b, HG_DIAG // 2, HG_DIAG, HG_DIAG)
    pairs.append((q * jnp.exp2(jnp.minimum(dd, HG_EXP_BOUND)).astype(BF16),
                  kk * jnp.exp2(jnp.minimum(-dd, HG_EXP_BOUND)).astype(BF16)))
    return q_in, k_st, decay, pairs, v


def _hgrn_attend(factors, hs, st, masks):
    q_in, k_st, decay, pairs, v = factors
    o_inter = _dot_nt(q_in[:, hs], st.astype(BF16))
    st_next = decay[:, hs] * st + _dot_tn(v[:, hs], k_st[:, hs])
    attn = jnp.zeros((HG_CHUNK, HG_CHUNK), F32)
    for (qw, kw), mask in zip(pairs, masks):
        attn = jnp.where(mask, _dot_nt(qw[:, hs], kw[:, hs]), attn)
    return o_inter, st_next, attn.astype(BF16)


def _hgrn_kernel(q_ref, k_ref, lf_ref, v_ref, g_ref, gain_ref, o_ref, st_ref):
    tri, masks = _hgrn_tables(HG_CHUNK)
    gain = gain_ref[...]
    d = st_ref.shape[1]
    st_ref[...] = jnp.zeros_like(st_ref)

    def body(ti, carry):
        base = ti * (HG_UNROLL * HG_CHUNK)
        rows = [pl.ds(pl.multiple_of(base + u * HG_CHUNK, HG_CHUNK), HG_CHUNK) for u in range(HG_UNROLL)]
        states = [st_ref[h] for h in range(HG_HEADS)]
        factors = {0: _hgrn_factors(rows[0], q_ref, k_ref, lf_ref, v_ref, tri)}
        pieces = {u: [] for u in range(HG_UNROLL)}

        def finish(u, hs, o_inter, attn):
            o = o_inter + _dot(attn, factors[u][4][:, hs])
            pieces[u].append(o * lax.rsqrt(jnp.mean(o * o, axis=-1, keepdims=True) + EPS) * gain)
            if len(pieces[u]) == HG_HEADS:
                out = jnp.concatenate(pieces[u], axis=1) * g_ref[0, rows[u], :].astype(F32)
                o_ref[0, rows[u], :] = out.astype(o_ref.dtype)

        pending = None
        for u in range(HG_UNROLL):
            if u + 1 < HG_UNROLL:
                factors[u + 1] = _hgrn_factors(rows[u + 1], q_ref, k_ref, lf_ref, v_ref, tri)
            for h in range(HG_HEADS):
                hs = slice(h * d, (h + 1) * d)
                o_inter, states[h], attn = _hgrn_attend(factors[u], hs, states[h], masks)
                if pending is not None:
                    finish(*pending)
                pending = (u, hs, o_inter, attn)
        finish(*pending)
        for h in range(HG_HEADS):
            st_ref[h] = states[h]
        return carry

    lax.fori_loop(0, q_ref.shape[1] // (HG_UNROLL * HG_CHUNK), body, 0)


def _hgrn(hq, kk, lf, hv, hg, gain):
    bsz, t, hw = hq.shape
    d = hw // HG_HEADS
    blk = pl.BlockSpec((1, t, hw), lambda b: (b, 0, 0))
    return pl.pallas_call(
        _hgrn_kernel,
        out_shape=jax.ShapeDtypeStruct((bsz, t, hw), BF16),
        grid=(bsz,),
        in_specs=[blk, blk, blk, blk, blk, pl.BlockSpec((1, d), lambda b: (0, 0))],
        out_specs=blk,
        scratch_shapes=[pltpu.VMEM((HG_HEADS, d, d), F32)],
        compiler_params=pltpu.CompilerParams(dimension_semantics=("parallel",), vmem_limit_bytes=VMEM_LIMIT),
        name="hgrn",
    )(hq, kk, lf, hv, hg, gain.reshape(1, d))


SLOPE_PARTS = 3
AUG_ROWS = 8
QUERY_TILE = 2 * MXU_TILE


def _moba_prepare(a, slope_ref, pair, qt, k_ref, vt_ref, gate_ref, kaug_scr, qaug_scr, vaug_scr):
    t_len, width = k_ref.shape[1], k_ref.shape[2]
    nb = t_len // BLOCK
    dh = width // 2
    k = k_ref[0]
    row = lax.broadcasted_iota(jnp.int32, k.shape, 0)
    ln = lax.broadcasted_iota(jnp.int32, k.shape, 1)
    in_block = (row % BLOCK).astype(F32)
    block_start = (row - row % BLOCK).astype(F32)
    feat = lax.broadcasted_iota(jnp.int32, (width, t_len), 0)
    jrow = lax.broadcasted_iota(jnp.int32, (nb, t_len), 0)
    qblk = lax.broadcasted_iota(jnp.int32, (nb, t_len), 1) // BLOCK
    crow = lax.broadcasted_iota(jnp.int32, (AUG_ROWS, t_len), 0)
    x0 = (1 - a) * dh
    y0 = x0 + AUG_ROWS
    extra = jnp.where(ln < x0 + SLOPE_PARTS, in_block,
                      jnp.where(ln < x0 + 2 * SLOPE_PARTS, block_start,
                                (ln - y0 == row // BLOCK).astype(F32)))
    kaug_scr[a] = jnp.where((ln >= x0) & (ln < y0 + nb), extra.astype(BF16), k)
    vaug_scr[a] = jnp.concatenate([vt_ref[0, a * dh:(a + 1) * dh, :], jnp.ones((BF16_ROWS, t_len), BF16)], axis=0)

    g = jnp.where(jrow < qblk, gate_ref[0, a * nb:(a + 1) * nb, :], -jnp.inf)
    rank = jnp.zeros((nb, t_len), jnp.int32)
    for i in range(nb - 1):
        gi = g[i:i + 1, :]
        rank = rank + ((gi > g) | ((gi == g) & (i < jrow))).astype(jnp.int32)
    sel_t = jnp.where((rank < TOPK) | (jrow >= qblk), 0.0, NEG)

    slope_t = jnp.zeros((AUG_ROWS, t_len), F32)
    for i in range(SLOPE_PARTS):
        slope_t = jnp.where((crow == i) | (crow == SLOPE_PARTS + i), slope_ref[i, 2 * pair + a], slope_t)
    pieces = [slope_t, sel_t, jnp.zeros((width - x0 - AUG_ROWS - nb, t_len), F32)]
    if x0:
        pieces.insert(0, jnp.zeros((x0, t_len), F32))
    own = (feat >= a * dh) & (feat < (a + 1) * dh)
    qaug_scr[a] = jnp.where(own, qt, jnp.concatenate(pieces, axis=0)).astype(BF16)


def _moba_scores(lo, a, kaug_scr, qaug_scr):
    hi = lo + QUERY_TILE
    key = lax.broadcasted_iota(jnp.int32, (QUERY_TILE, QUERY_TILE), 0)
    qry = lax.broadcasted_iota(jnp.int32, (QUERY_TILE, QUERY_TILE), 1)
    s = _dot(kaug_scr[a, 0:hi, :], qaug_scr[a, :, lo:hi])
    s_own = jnp.where(key <= qry, s[lo:], NEG)
    return s_own if lo == 0 else jnp.concatenate([s[:lo], s_own], axis=0)


def _moba_probs(s):
    m = jnp.max(s, axis=0, keepdims=True)
    return jnp.exp2(s - m).astype(BF16)


def _moba_values(lo, a, p, vaug_scr):
    dh = vaug_scr.shape[1] - BF16_ROWS
    acc = _dot(vaug_scr[a, :, 0:lo + QUERY_TILE], p)
    return acc[0:dh] / acc[dh:dh + 1]


def _moba_kernel(slope_ref, q_ref, k_ref, vt_ref, gate_ref, o_ref, kaug_scr, qaug_scr, vaug_scr):
    qt = q_ref[0].astype(F32).T
    for a in range(2):
        _moba_prepare(a, slope_ref, pl.program_id(1), qt, k_ref, vt_ref, gate_ref, kaug_scr, qaug_scr, vaug_scr)
    chains = [(lo, a) for lo in range(0, k_ref.shape[1], QUERY_TILE) for a in range(2)]
    scores, probs, outs = {}, {}, {}
    for i in range(len(chains) + 2):
        if i < len(chains):
            scores[i] = _moba_scores(*chains[i], kaug_scr, qaug_scr)
        if 1 <= i <= len(chains):
            probs[i - 1] = _moba_probs(scores.pop(i - 1))
        if i >= 2:
            lo, a = chains[i - 2]
            outs[a] = _moba_values(lo, a, probs.pop(i - 2), vaug_scr)
            if a == 1:
                o_ref[0, lo:lo + QUERY_TILE, :] = jnp.concatenate([outs[0], outs[1]], axis=0).T.astype(o_ref.dtype)


def _moba(aq, ak, av_t, gate_t):
    bsz, t, aw = aq.shape
    dh = aw // ATT_HEADS
    nb = t // BLOCK
    assert nb % 8 == 0 and AUG_ROWS + nb <= dh and 2 * SLOPE_PARTS <= AUG_ROWS and t % QUERY_TILE == 0
    slopes = LOG2E * jnp.exp2(-8.0 * jnp.arange(1, ATT_HEADS + 1, dtype=F32) / ATT_HEADS)
    parts = []
    for _ in range(SLOPE_PARTS):
        parts.append(slopes.astype(BF16).astype(F32))
        slopes = slopes - parts[-1]
    pair_spec = pl.BlockSpec((1, t, 2 * dh), lambda b, p: (b, 0, p))
    return pl.pallas_call(
        _moba_kernel,
        out_shape=jax.ShapeDtypeStruct((bsz, t, aw), BF16),
        grid=(bsz, ATT_HEADS // 2),
        in_specs=[pl.BlockSpec(memory_space=pltpu.SMEM),
                  pair_spec, pair_spec,
                  pl.BlockSpec((1, 2 * dh, t), lambda b, p: (b, p, 0)),
                  pl.BlockSpec((1, 2 * nb, t), lambda b, p: (b, p, 0))],
        out_specs=pair_spec,
        scratch_shapes=[pltpu.VMEM((2, t, 2 * dh), BF16), pltpu.VMEM((2, 2 * dh, t), BF16),
                        pltpu.VMEM((2, dh + BF16_ROWS, t), BF16)],
        compiler_params=pltpu.CompilerParams(
            dimension_semantics=("parallel", "parallel"), vmem_limit_bytes=VMEM_LIMIT),
        name="moba",
    )(jnp.stack(parts), aq, ak, av_t, gate_t)


def _ffn_kernel(x_ref, mh_ref, ma_ref, mod_ref, g2_ref, wo_ref, wg_ref, wu_ref, wd_ref, o_ref, *, ff_chunks):
    hw = mh_ref.shape[2]
    mod = mod_ref[0]
    tiles = wg_ref.shape[1] // MXU_TILE
    bounds = [-(-tiles * ci // ff_chunks) * MXU_TILE for ci in range(ff_chunks)] + [wg_ref.shape[1]]
    mixed = _dot(mh_ref[0], wo_ref[0:hw]) + _dot(ma_ref[0], wo_ref[hw:])
    x1 = x_ref[0] + mod[2:3] * mixed
    xn = x1 * lax.rsqrt(jnp.mean(x1 * x1, axis=-1, keepdims=True) + EPS)
    hb = ((xn * g2_ref[...]) * (1.0 + mod[4:5]) + mod[3:4]).astype(BF16)
    spans = list(zip(bounds[:-1], bounds[1:]))
    up = lambda lo, hi: (_dot(hb, wg_ref[:, lo:hi]), _dot(hb, wu_ref[:, lo:hi]))
    nxt = up(*spans[0])
    y = jnp.zeros_like(x1)
    for ci, (lo, hi) in enumerate(spans):
        gate, lin = nxt
        if ci + 1 < len(spans):
            nxt = up(*spans[ci + 1])
        y = y + _dot((_silu(gate) * lin).astype(BF16), wd_ref[lo:hi])
    o_ref[0] = x1 + mod[5:6] * y


def _ffn(x, mh, ma, mod, g2, w_out, w_gate, w_up, w_down, *, tm=512, ff_chunks=2):
    bsz, t, d = x.shape
    tok = lambda w: pl.BlockSpec((1, tm, w), lambda b, i: (b, i, 0))
    const = lambda shape: pl.BlockSpec(shape, lambda b, i: (0,) * len(shape), pipeline_mode=pl.Buffered(1))
    return pl.pallas_call(
        functools.partial(_ffn_kernel, ff_chunks=ff_chunks),
        out_shape=jax.ShapeDtypeStruct((bsz, t, d), x.dtype),
        grid=(bsz, t // tm),
        in_specs=[tok(d), tok(mh.shape[2]), tok(ma.shape[2]),
                  pl.BlockSpec((1, 6, d), lambda b, i: (b, 0, 0)),
                  const((1, d)), const(w_out.shape), const(w_gate.shape), const(w_up.shape), const(w_down.shape)],
        out_specs=tok(d),
        compiler_params=pltpu.CompilerParams(
            dimension_semantics=("parallel", "parallel"), vmem_limit_bytes=VMEM_LIMIT),
        name="ffn",
    )(x, mh, ma, mod, g2.reshape(1, d), w_out, w_gate, w_up, w_down)


def kernel(x, c, w_ada, b_ada, norm1_g, w_in, lb_logits, hg_norm_g, q_norm_g, k_norm_g,
           w_out, norm2_g, w_gate, w_up, w_down):
    bsz, _, d = x.shape
    for l in range(w_ada.shape[0]):
        mod = _adaln(c, w_ada[l], b_ada[l]).reshape(bsz, 6, d)
        (hq, kk, lf, hv, hg, aq, ak, av_t, gate_t), ffn_weights = _inproj(
            x, mod, norm1_g[l], w_in[l], lb_logits, q_norm_g[l], k_norm_g[l],
            (w_out[l], w_gate[l], w_up[l], w_down[l]), layer=l)
        o_hg = _hgrn(hq, kk, lf, hv, hg, hg_norm_g[l])
        o_att = _moba(aq, ak, av_t, gate_t)
        x = _ffn(x, o_hg, o_att, mod, norm2_g[l], *ffn_weights)
    return x
```

```python
import functools

import jax
import jax.numpy as jnp
from jax import lax
from jax.experimental import pallas as pl
from jax.experimental.pallas import tpu as pltpu

F32 = jnp.float32
BF16 = jnp.bfloat16

HG_HEADS = 4
ATT_HEADS = 8
BLOCK = 256
TOPK = 3
HG_CHUNK = 256
EPS = 1e-6
NEG = -(2.0 ** 100)

LOG2E = 1.4426950408889634
MXU_TILE = 256

VMEM_LIMIT = 56 * 1024 * 1024


def _silu(t):
    return t * jax.nn.sigmoid(t)


def _dot(a, b):
    return jnp.dot(a, b, preferred_element_type=F32)


def _dot_nt(a, b):
    return lax.dot_general(a, b, (((1,), (1,)), ((), ())), preferred_element_type=F32)


def _dot_tn(a, b):
    return lax.dot_general(a, b, (((0,), (0,)), ((), ())), preferred_element_type=F32)


def _split2(t):
    hi = t.astype(BF16)
    lo = (t - hi.astype(F32)).astype(BF16)
    return hi, lo


def _dot3(a, b, dot=_dot):
    a_hi, a_lo = _split2(a)
    b_hi, b_lo = _split2(b)
    return (dot(a_hi, b_lo) + dot(a_lo, b_hi)) + dot(a_hi, b_hi)


def _adaln_kernel(c_ref, w_ref, b_ref, o_ref):
    o_ref[...] = _dot3(_silu(c_ref[...]), w_ref[...]) + b_ref[...]


def _adaln(c, w, b):
    bsz, d = c.shape
    n = w.shape[1]
    tn = 1024
    return pl.pallas_call(
        _adaln_kernel,
        out_shape=jax.ShapeDtypeStruct((bsz, n), F32),
        grid=(n // tn,),
        in_specs=[pl.BlockSpec((bsz, d), lambda j: (0, 0)),
                  pl.BlockSpec((d, tn), lambda j: (0, j)),
                  pl.BlockSpec((1, tn), lambda j: (0, j))],
        out_specs=pl.BlockSpec((bsz, tn), lambda j: (0, j)),
        name="adaln",
    )(c, w, b.reshape(1, n))


def _inproj_kernel(x_ref, mod_ref, g1_ref, w_ref, lbl_ref, qg_ref, kg_ref, pool_ref,
                   f32_0, f32_1, f32_2, f32_3,
                   hq_ref, kk_ref, lf_ref, hv_ref, hg_ref, aq_ref, ak_ref, avt_ref, gate_ref,
                   bf16_0, bf16_1, bf16_2, bf16_3,
                   kmean_scr, *, tm, hw, aw, layer):
    i = pl.program_id(1)

    @pl.when(i == 0)
    def _():
        kmean_scr[...] = jnp.zeros_like(kmean_scr)

    for src, dst in ((f32_0, bf16_0), (f32_1, bf16_1), (f32_2, bf16_2), (f32_3, bf16_3)):
        dst[...] = src[...].astype(BF16)

    dh = aw // ATT_HEADS
    nb = kmean_scr.shape[0]

    x = x_ref[0]
    xn = x * lax.rsqrt(jnp.mean(x * x, axis=-1, keepdims=True) + EPS)
    mod = mod_ref[0]
    h = (xn * g1_ref[...]) * (1.0 + mod[1:2]) + mod[0:1]
    hb = h.astype(BF16)

    def proj(lo, width):
        return _dot(hb, w_ref[:, lo:lo + width])

    def head_norm(t, gain):
        t2 = (t * t).astype(BF16)
        pw = pool_ref.shape[0]
        ms = jnp.concatenate([_dot(t2[:, c:c + pw], pool_ref[...]) for c in range(0, aw, pw)], axis=1)
        return t * lax.rsqrt(ms + EPS) * gain

    p_k = proj(4 * hw + aw, aw)
    p_q = proj(4 * hw, aw)
    p_v = proj(4 * hw + 2 * aw, aw)
    kn = head_norm(p_k, kg_ref[...])
    ak_ref[0] = kn.astype(BF16)
    per_tile = tm // BLOCK
    for s in range(per_tile):
        kmean_scr[pl.ds(i * per_tile + s, 1), :] = jnp.mean(
            kn[s * BLOCK:(s + 1) * BLOCK], axis=0, keepdims=True)

    p_hq = proj(0, hw)
    qn = head_norm(p_q, qg_ref[...])
    aq_ref[0] = (qn * (dh ** -0.5 * LOG2E)).astype(BF16)

    p_hf = proj(hw, hw)
    kmean = kmean_scr[...]
    rows = lax.broadcasted_iota(jnp.int32, (ATT_HEADS * nb, aw), 0)
    cols = lax.broadcasted_iota(jnp.int32, (ATT_HEADS * nb, aw), 1)
    sel = jnp.where(rows // nb == cols // dh, jnp.concatenate([kmean] * ATT_HEADS, axis=0), 0.0)
    sel_hi, sel_lo = _split2(sel)
    q_hi, q_lo = _split2(qn)
    both = _dot_nt(jnp.concatenate([sel_hi, sel_lo], axis=0), q_hi)
    n_sel = sel.shape[0]
    gate_ref[0] = (_dot_nt(sel_hi, q_lo) + both[n_sel:]) + both[:n_sel]
    avt_ref[0] = p_v.T.astype(BF16)

    p_hg = proj(3 * hw, hw)
    hq_ref[0] = _silu(p_hq).astype(BF16)
    lbl = lbl_ref[...]
    e = jnp.exp(lbl - jnp.max(lbl, axis=0, keepdims=True))
    lb = jnp.sum(e[0:layer + 1], axis=0, keepdims=True) / jnp.sum(e, axis=0, keepdims=True)
    p_hv = proj(2 * hw, hw)
    f = lb + (1.0 - lb) * jax.nn.sigmoid(p_hf)
    lf_ref[0] = jnp.log2(f)
    kk_ref[0] = (1.0 - f).astype(BF16)
    hg_ref[0] = _silu(p_hg).astype(BF16)
    hv_ref[0] = p_hv.astype(BF16)


def _cast_rows(rows, steps):
    chunk = BF16_ROWS
    while chunk * steps < rows or rows % chunk:
        chunk += BF16_ROWS
    return chunk


def _inproj(x, mod, g1, w_in, lb_logits, qg, kg, later_weights, *, layer, tm=512):
    bsz, t, d = x.shape
    hw = lb_logits.shape[1]
    aw = (w_in.shape[1] - 4 * hw) // 3
    dh = aw // ATT_HEADS
    nb = t // BLOCK
    tiles = t // tm
    lanes = jnp.arange(MXU_TILE)
    pool = jnp.where(lanes[:, None] // dh == lanes[None, :] // dh, 1.0 / dh, 0.0).astype(BF16)
    tok = lambda w: pl.BlockSpec((1, tm, w), lambda b, i: (b, i, 0))
    const = lambda shape: pl.BlockSpec(shape, lambda b, i: (0,) * len(shape), pipeline_mode=pl.Buffered(1))
    bf = lambda w: jax.ShapeDtypeStruct((bsz, t, w), BF16)
    tok_t = lambda rows: pl.BlockSpec((1, rows, tm), lambda b, i: (b, 0, i))

    def cast_spec(w):
        rows = _cast_rows(w.shape[0], bsz * tiles)
        last = w.shape[0] // rows - 1
        return pl.BlockSpec((rows, w.shape[1]), lambda b, i: (jnp.minimum(b * tiles + i, last), 0))

    cast_specs = [cast_spec(w) for w in later_weights]
    outs = pl.pallas_call(
        functools.partial(_inproj_kernel, tm=tm, hw=hw, aw=aw, layer=layer),
        out_shape=[bf(hw), bf(hw), jax.ShapeDtypeStruct((bsz, t, hw), F32), bf(hw), bf(hw),
                   bf(aw), bf(aw), jax.ShapeDtypeStruct((bsz, aw, t), BF16),
                   jax.ShapeDtypeStruct((bsz, ATT_HEADS * nb, t), F32)]
                  + [jax.ShapeDtypeStruct(w.shape, BF16) for w in later_weights],
        grid=(bsz, tiles),
        in_specs=[tok(d),
                  pl.BlockSpec((1, 6, d), lambda b, i: (b, 0, 0)),
                  const((1, d)), const(w_in.shape), const(lb_logits.shape),
                  const((1, aw)), const((1, aw)), const((MXU_TILE, MXU_TILE))] + cast_specs,
        out_specs=[tok(hw), tok(hw), tok(hw), tok(hw), tok(hw), tok(aw), tok(aw), tok_t(aw),
                   tok_t(ATT_HEADS * nb)] + cast_specs,
        scratch_shapes=[pltpu.VMEM((nb, aw), F32)],
        compiler_params=pltpu.CompilerParams(
            dimension_semantics=("arbitrary", "arbitrary"), vmem_limit_bytes=VMEM_LIMIT),
        name="inproj",
    )(x, mod, g1.reshape(1, d), w_in.astype(BF16), lb_logits,
      jnp.tile(qg, ATT_HEADS).reshape(1, aw), jnp.tile(kg, ATT_HEADS).reshape(1, aw), pool, *later_weights)
    return outs[:9], outs[9:]


HG_LEVELS = (128, 64, 32, 16, 8)
HG_DIAG = 8
HG_EXP_BOUND = 100.0
HG_UNROLL = 8
BF16_ROWS = 16


def _group_rows(b, first, step, rows):
    n = b.shape[0] // rows
    return jnp.concatenate(
        [jnp.broadcast_to(b[first + g * step:first + g * step + 1], (rows, b.shape[1])) for g in range(n)], axis=0)


def _blend_rows(q, kk, m):
    n = q.shape[0] // m
    return jnp.concatenate([(q if g % 2 else kk)[g * m:(g + 1) * m] for g in range(n)], axis=0)


def _hgrn_tables(c_len):
    r = lax.broadcasted_iota(jnp.int32, (c_len, c_len), 0)
    c = lax.broadcasted_iota(jnp.int32, (c_len, c_len), 1)
    tri = (r >= c).astype(BF16)
    masks = [(r // (2 * m) == c // (2 * m)) & ((r // m) % 2 == 1) & ((c // m) % 2 == 0) for m in HG_LEVELS]
    masks.append((r // HG_DIAG == c // HG_DIAG) & (r >= c))
    return tri, masks


def _hgrn_factors(rows, q_ref, k_ref, lf_ref, v_ref, tri):
    c_len = HG_CHUNK
    q = q_ref[0, rows, :]
    kk = k_ref[0, rows, :]
    lf = lf_ref[0, rows, :]
    v = v_ref[0, rows, :]

    l1 = lf.astype(BF16)
    l2 = (lf - l1.astype(F32)).astype(BF16)
    b = _dot(tri, l2) + _dot(tri, l1)
    b_last = b[c_len - 1:c_len]

    q_in = q * jnp.exp2(b).astype(BF16)
    k_st = kk * jnp.exp2(b_last - b).astype(BF16)
    decay = jnp.exp2(b_last)
    pairs = []
    for m in HG_LEVELS:
        w = jnp.exp2(-jnp.abs(b - _group_rows(b, m - 1, 2 * m, 2 * m)))
        if m % BF16_ROWS == 0:
            x = _blend_rows(q, kk, m) * w.astype(BF16)
        else:
            x = (_blend_rows(q.astype(F32), kk.astype(F32), m) * w).astype(BF16)
        pairs.append((x, x))
    dd = b - _group_rows(b, HG_DIAG // 2, HG_DIAG, HG_DIAG)
    pairs.append((q * jnp.exp2(jnp.minimum(dd, HG_EXP_BOUND)).astype(BF16),
                  kk * jnp.exp2(jnp.minimum(-dd, HG_EXP_BOUND)).astype(BF16)))
    return q_in, k_st, decay, pairs, v


def _hgrn_attend(factors, hs, st, masks):
    q_in, k_st, decay, pairs, v = factors
    o_inter = _dot_nt(q_in[:, hs], st.astype(BF16))
    st_next = decay[:, hs] * st + _dot_tn(v[:, hs], k_st[:, hs])
    attn = jnp.zeros((HG_CHUNK, HG_CHUNK), F32)
    for (qw, kw), mask in zip(pairs, masks):
        attn = jnp.where(mask, _dot_nt(qw[:, hs], kw[:, hs]), attn)
    return o_inter, st_next, attn.astype(BF16)


def _hgrn_kernel(q_ref, k_ref, lf_ref, v_ref, g_ref, gain_ref, o_ref, st_ref):
    tri, masks = _hgrn_tables(HG_CHUNK)
    gain = gain_ref[...]
    d = st_ref.shape[1]
    st_ref[...] = jnp.zeros_like(st_ref)

    def body(ti, carry):
        base = ti * (HG_UNROLL * HG_CHUNK)
        rows = [pl.ds(pl.multiple_of(base + u * HG_CHUNK, HG_CHUNK), HG_CHUNK) for u in range(HG_UNROLL)]
        states = [st_ref[h] for h in range(HG_HEADS)]
        factors = {0: _hgrn_factors(rows[0], q_ref, k_ref, lf_ref, v_ref, tri)}
        pieces = {u: [] for u in range(HG_UNROLL)}

        def finish(u, hs, o_inter, attn):
            o = o_inter + _dot(attn, factors[u][4][:, hs])
            pieces[u].append(o * lax.rsqrt(jnp.mean(o * o, axis=-1, keepdims=True) + EPS) * gain)
            if len(pieces[u]) == HG_HEADS:
                out = jnp.concatenate(pieces[u], axis=1) * g_ref[0, rows[u], :].astype(F32)
                o_ref[0, rows[u], :] = out.astype(o_ref.dtype)

        pending = None
        for u in range(HG_UNROLL):
            if u + 1 < HG_UNROLL:
                factors[u + 1] = _hgrn_factors(rows[u + 1], q_ref, k_ref, lf_ref, v_ref, tri)
            for h in range(HG_HEADS):
                hs = slice(h * d, (h + 1) * d)
                o_inter, states[h], attn = _hgrn_attend(factors[u], hs, states[h], masks)
                if pending is not None:
                    finish(*pending)
                pending = (u, hs, o_inter, attn)
        finish(*pending)
        for h in range(HG_HEADS):
            st_ref[h] = states[h]
        return carry

    lax.fori_loop(0, q_ref.shape[1] // (HG_UNROLL * HG_CHUNK), body, 0)


def _hgrn(hq, kk, lf, hv, hg, gain):
    bsz, t, hw = hq.shape
    d = hw // HG_HEADS
    blk = pl.BlockSpec((1, t, hw), lambda b: (b, 0, 0))
    return pl.pallas_call(
        _hgrn_kernel,
        out_shape=jax.ShapeDtypeStruct((bsz, t, hw), BF16),
        grid=(bsz,),
        in_specs=[blk, blk, blk, blk, blk, pl.BlockSpec((1, d), lambda b: (0, 0))],
        out_specs=blk,
        scratch_shapes=[pltpu.VMEM((HG_HEADS, d, d), F32)],
        compiler_params=pltpu.CompilerParams(dimension_semantics=("parallel",), vmem_limit_bytes=VMEM_LIMIT),
        name="hgrn",
    )(hq, kk, lf, hv, hg, gain.reshape(1, d))


SLOPE_PARTS = 3
AUG_ROWS = 8
QUERY_TILE = 2 * MXU_TILE


def _moba_prepare(a, slope_ref, pair, qt, k_ref, vt_ref, gate_ref, kaug_scr, qaug_scr, vaug_scr):
    t_len, width = k_ref.shape[1], k_ref.shape[2]
    nb = t_len // BLOCK
    dh = width // 2
    k = k_ref[0]
    row = lax.broadcasted_iota(jnp.int32, k.shape, 0)
    ln = lax.broadcasted_iota(jnp.int32, k.shape, 1)
    in_block = (row % BLOCK).astype(F32)
    block_start = (row - row % BLOCK).astype(F32)
    feat = lax.broadcasted_iota(jnp.int32, (width, t_len), 0)
    jrow = lax.broadcasted_iota(jnp.int32, (nb, t_len), 0)
    qblk = lax.broadcasted_iota(jnp.int32, (nb, t_len), 1) // BLOCK
    crow = lax.broadcasted_iota(jnp.int32, (AUG_ROWS, t_len), 0)
    x0 = (1 - a) * dh
    y0 = x0 + AUG_ROWS
    extra = jnp.where(ln < x0 + SLOPE_PARTS, in_block,
                      jnp.where(ln < x0 + 2 * SLOPE_PARTS, block_start,
                                (ln - y0 == row // BLOCK).astype(F32)))
    kaug_scr[a] = jnp.where((ln >= x0) & (ln < y0 + nb), extra.astype(BF16), k)
    vaug_scr[a] = jnp.concatenate([vt_ref[0, a * dh:(a + 1) * dh, :], jnp.ones((BF16_ROWS, t_len), BF16)], axis=0)

    g = jnp.where(jrow < qblk, gate_ref[0, a * nb:(a + 1) * nb, :], -jnp.inf)
    rank = jnp.zeros((nb, t_len), jnp.int32)
    for i in range(nb - 1):
        gi = g[i:i + 1, :]
        rank = rank + ((gi > g) | ((gi == g) & (i < jrow))).astype(jnp.int32)
    sel_t = jnp.where((rank < TOPK) | (jrow >= qblk), 0.0, NEG)

    slope_t = jnp.zeros((AUG_ROWS, t_len), F32)
    for i in range(SLOPE_PARTS):
        slope_t = jnp.where((crow == i) | (crow == SLOPE_PARTS + i), slope_ref[i, 2 * pair + a], slope_t)
    pieces = [slope_t, sel_t, jnp.zeros((width - x0 - AUG_ROWS - nb, t_len), F32)]
    if x0:
        pieces.insert(0, jnp.zeros((x0, t_len), F32))
    own = (feat >= a * dh) & (feat < (a + 1) * dh)
    qaug_scr[a] = jnp.where(own, qt, jnp.concatenate(pieces, axis=0)).astype(BF16)


def _moba_scores(lo, a, kaug_scr, qaug_scr):
    hi = lo + QUERY_TILE
    key = lax.broadcasted_iota(jnp.int32, (QUERY_TILE, QUERY_TILE), 0)
    qry = lax.broadcasted_iota(jnp.int32, (QUERY_TILE, QUERY_TILE), 1)
    s = _dot(kaug_scr[a, 0:hi, :], qaug_scr[a, :, lo:hi])
    s_own = jnp.where(key <= qry, s[lo:], NEG)
    return s_own if lo == 0 else jnp.concatenate([s[:lo], s_own], axis=0)


def _moba_probs(s):
    m = jnp.max(s, axis=0, keepdims=True)
    return jnp.exp2(s - m).astype(BF16)


def _moba_values(lo, a, p, vaug_scr):
    dh = vaug_scr.shape[1] - BF16_ROWS
    acc = _dot(vaug_scr[a, :, 0:lo + QUERY_TILE], p)
    return acc[0:dh] / acc[dh:dh + 1]


def _moba_kernel(slope_ref, q_ref, k_ref, vt_ref, gate_ref, o_ref, kaug_scr, qaug_scr, vaug_scr):
    qt = q_ref[0].astype(F32).T
    for a in range(2):
        _moba_prepare(a, slope_ref, pl.program_id(1), qt, k_ref, vt_ref, gate_ref, kaug_scr, qaug_scr, vaug_scr)
    chains = [(lo, a) for lo in range(0, k_ref.shape[1], QUERY_TILE) for a in range(2)]
    scores, probs, outs = {}, {}, {}
    for i in range(len(chains) + 2):
        if i < len(chains):
            scores[i] = _moba_scores(*chains[i], kaug_scr, qaug_scr)
        if 1 <= i <= len(chains):
            probs[i - 1] = _moba_probs(scores.pop(i - 1))
        if i >= 2:
            lo, a = chains[i - 2]
            outs[a] = _moba_values(lo, a, probs.pop(i - 2), vaug_scr)
            if a == 1:
                o_ref[0, lo:lo + QUERY_TILE, :] = jnp.concatenate([outs[0], outs[1]], axis=0).T.astype(o_ref.dtype)


def _moba(aq, ak, av_t, gate_t):
    bsz, t, aw = aq.shape
    dh = aw // ATT_HEADS
    nb = t // BLOCK
    assert nb % 8 == 0 and AUG_ROWS + nb <= dh and 2 * SLOPE_PARTS <= AUG_ROWS and t % QUERY_TILE == 0
    slopes = LOG2E * jnp.exp2(-8.0 * jnp.arange(1, ATT_HEADS + 1, dtype=F32) / ATT_HEADS)
    parts = []
    for _ in range(SLOPE_PARTS):
        parts.append(slopes.astype(BF16).astype(F32))
        slopes = slopes - parts[-1]
    pair_spec = pl.BlockSpec((1, t, 2 * dh), lambda b, p: (b, 0, p))
    return pl.pallas_call(
        _moba_kernel,
        out_shape=jax.ShapeDtypeStruct((bsz, t, aw), BF16),
        grid=(bsz, ATT_HEADS // 2),
        in_specs=[pl.BlockSpec(memory_space=pltpu.SMEM),
                  pair_spec, pair_spec,
                  pl.BlockSpec((1, 2 * dh, t), lambda b, p: (b, p, 0)),
                  pl.BlockSpec((1, 2 * nb, t), lambda b, p: (b, p, 0))],
        out_specs=pair_spec,
        scratch_shapes=[pltpu.VMEM((2, t, 2 * dh), BF16), pltpu.VMEM((2, 2 * dh, t), BF16),
                        pltpu.VMEM((2, dh + BF16_ROWS, t), BF16)],
        compiler_params=pltpu.CompilerParams(
            dimension_semantics=("parallel", "parallel"), vmem_limit_bytes=VMEM_LIMIT),
        name="moba",
    )(jnp.stack(parts), aq, ak, av_t, gate_t)


def _ffn_kernel(x_ref, mh_ref, ma_ref, mod_in_ref, mod_out_ref, g2_ref, wo_ref, wg_ref, wu_ref, wd_ref, o_ref,
                x1_a, hb_a, x1_b, hb_b, *, ff_chunks, n_tiles):
    s = pl.program_id(0)
    hw = mh_ref.shape[2]
    tiles = wg_ref.shape[1] // MXU_TILE
    bounds = [-(-tiles * ci // ff_chunks) * MXU_TILE for ci in range(ff_chunks)] + [wg_ref.shape[1]]
    spans = list(zip(bounds[:-1], bounds[1:]))

    def normalise(x1_scr, hb_scr):
        mod = mod_in_ref[0]
        mixed = _dot(mh_ref[0], wo_ref[0:hw]) + _dot(ma_ref[0], wo_ref[hw:])
        x1 = x_ref[0] + mod[2:3] * mixed
        xn = x1 * lax.rsqrt(jnp.mean(x1 * x1, axis=-1, keepdims=True) + EPS)
        x1_scr[...] = x1
        hb_scr[...] = ((xn * g2_ref[...]) * (1.0 + mod[4:5]) + mod[3:4]).astype(BF16)

    def swiglu(x1_scr, hb_scr):
        hb = hb_scr[...]
        y = jnp.zeros(x1_scr.shape, F32)
        for lo, hi in spans:
            act = _silu(_dot(hb, wg_ref[:, lo:hi])) * _dot(hb, wu_ref[:, lo:hi])
            y = y + _dot(act.astype(BF16), wd_ref[lo:hi])
        o_ref[0] = x1_scr[...] + mod_out_ref[0][5:6] * y

    @pl.when(s == 0)
    def _():
        normalise(x1_a, hb_a)

    @pl.when((s > 0) & (s < n_tiles) & (s % 2 == 1))
    def _():
        normalise(x1_b, hb_b)
        swiglu(x1_a, hb_a)

    @pl.when((s > 0) & (s < n_tiles) & (s % 2 == 0))
    def _():
        normalise(x1_a, hb_a)
        swiglu(x1_b, hb_b)

    @pl.when(s == n_tiles)
    def _():
        if n_tiles % 2:
            swiglu(x1_a, hb_a)
        else:
            swiglu(x1_b, hb_b)


def _ffn(x, mh, ma, mod, g2, w_out, w_gate, w_up, w_down, *, tm=512, ff_chunks=2):
    bsz, t, d = x.shape
    per_seq = t // tm
    n_tiles = bsz * per_seq
    t_in = lambda s: jnp.minimum(s, n_tiles - 1)
    t_out = lambda s: jnp.maximum(s - 1, 0)
    tok_in = lambda w: pl.BlockSpec((1, tm, w), lambda s: (t_in(s) // per_seq, t_in(s) % per_seq, 0))
    const = lambda shape: pl.BlockSpec(shape, lambda s: (0,) * len(shape), pipeline_mode=pl.Buffered(1))
    return pl.pallas_call(
        functools.partial(_ffn_kernel, ff_chunks=ff_chunks, n_tiles=n_tiles),
        out_shape=jax.ShapeDtypeStruct((bsz, t, d), x.dtype),
        grid=(n_tiles + 1,),
        in_specs=[tok_in(d), tok_in(mh.shape[2]), tok_in(ma.shape[2]),
                  pl.BlockSpec((1, 6, d), lambda s: (t_in(s) // per_seq, 0, 0)),
                  pl.BlockSpec((1, 6, d), lambda s: (t_out(s) // per_seq, 0, 0)),
                  const((1, d)), const(w_out.shape), const(w_gate.shape), const(w_up.shape), const(w_down.shape)],
        out_specs=pl.BlockSpec((1, tm, d), lambda s: (t_out(s) // per_seq, t_out(s) % per_seq, 0)),
        scratch_shapes=[pltpu.VMEM((tm, d), F32), pltpu.VMEM((tm, d), BF16),
                        pltpu.VMEM((tm, d), F32), pltpu.VMEM((tm, d), BF16)],
        compiler_params=pltpu.CompilerParams(
            dimension_semantics=("arbitrary",), vmem_limit_bytes=VMEM_LIMIT),
        name="ffn",
    )(x, mh, ma, mod, mod, g2.reshape(1, d), w_out, w_gate, w_up, w_down)


def kernel(x, c, w_ada, b_ada, norm1_g, w_in, lb_logits, hg_norm_g, q_norm_g, k_norm_g,
           w_out, norm2_g, w_gate, w_up, w_down):
    bsz, _, d = x.shape
    for l in range(w_ada.shape[0]):
        mod = _adaln(c, w_ada[l], b_ada[l]).reshape(bsz, 6, d)
        (hq, kk, lf, hv, hg, aq, ak, av_t, gate_t), ffn_weights = _inproj(
            x, mod, norm1_g[l], w_in[l], lb_logits, q_norm_g[l], k_norm_g[l],
            (w_out[l], w_gate[l], w_up[l], w_down[l]), layer=l)
        o_hg = _hgrn(hq, kk, lf, hv, hg, hg_norm_g[l])
        o_att = _moba(aq, ak, av_t, gate_t)
        x = _ffn(x, o_hg, o_att, mod, norm2_g[l], *ffn_weights)
    return x
```

```python
import functools

import jax
import jax.numpy as jnp
from jax import lax
from jax.experimental import pallas as pl
from jax.experimental.pallas import tpu as pltpu

F32 = jnp.float32
BF16 = jnp.bfloat16

HG_HEADS = 4
ATT_HEADS = 8
BLOCK = 256
TOPK = 3
HG_CHUNK = 256
EPS = 1e-6
NEG = -(2.0 ** 100)

LOG2E = 1.4426950408889634
MXU_TILE = 256

VMEM_LIMIT = 56 * 1024 * 1024


def _silu(t):
    return t * jax.nn.sigmoid(t)


def _dot(a, b):
    return jnp.dot(a, b, preferred_element_type=F32)


def _dot_nt(a, b):
    return lax.dot_general(a, b, (((1,), (1,)), ((), ())), preferred_element_type=F32)


def _dot_tn(a, b):
    return lax.dot_general(a, b, (((0,), (0,)), ((), ())), preferred_element_type=F32)


def _split2(t):
    hi = t.astype(BF16)
    lo = (t - hi.astype(F32)).astype(BF16)
    return hi, lo


def _dot3(a, b, dot=_dot):
    a_hi, a_lo = _split2(a)
    b_hi, b_lo = _split2(b)
    return (dot(a_hi, b_lo) + dot(a_lo, b_hi)) + dot(a_hi, b_hi)


def _adaln_kernel(c_ref, w_ref, b_ref, win_ref, o_ref, win_bf_ref):
    a_hi, a_lo = _split2(_silu(c_ref[...]))
    w_hi, w_lo = _split2(w_ref[...])
    rows = a_hi.shape[0]
    both = _dot(jnp.concatenate([a_hi, a_lo], axis=0), w_hi)
    o_ref[...] = ((_dot(a_hi, w_lo) + both[rows:]) + both[:rows]) + b_ref[...]
    win_bf_ref[...] = win_ref[...].astype(BF16)


def _adaln(c, w, b, w_in):
    bsz, d = c.shape
    n = w.shape[1]
    tn = 512
    steps = n // tn
    assert w_in.shape[1] % tn == 0 and w_in.shape[1] // tn <= steps
    last = w_in.shape[1] // tn - 1
    cast_spec = pl.BlockSpec((w_in.shape[0], tn), lambda j: (0, jnp.minimum(j, last)))
    return pl.pallas_call(
        _adaln_kernel,
        out_shape=[jax.ShapeDtypeStruct((bsz, n), F32), jax.ShapeDtypeStruct(w_in.shape, BF16)],
        grid=(steps,),
        in_specs=[pl.BlockSpec((bsz, d), lambda j: (0, 0)),
                  pl.BlockSpec((d, tn), lambda j: (0, j)),
                  pl.BlockSpec((1, tn), lambda j: (0, j)),
                  cast_spec],
        out_specs=[pl.BlockSpec((bsz, tn), lambda j: (0, j)), cast_spec],
        compiler_params=pltpu.CompilerParams(dimension_semantics=("arbitrary",)),
        name="adaln",
    )(c, w, b.reshape(1, n), w_in)


def _inproj_kernel(x_ref, mod_ref, g1_ref, w_ref, lbl_ref, qg_ref, kg_ref, pool_ref,
                   f32_0, f32_1, f32_2, f32_3,
                   hq_ref, kk_ref, lf_ref, hv_ref, hg_ref, aq_ref, ak_ref, avt_ref, gate_ref,
                   bf16_0, bf16_1, bf16_2, bf16_3,
                   kmean_scr, *, tm, hw, aw, layer):
    i = pl.program_id(1)

    @pl.when(i == 0)
    def _():
        kmean_scr[...] = jnp.zeros_like(kmean_scr)

    for src, dst in ((f32_0, bf16_0), (f32_1, bf16_1), (f32_2, bf16_2), (f32_3, bf16_3)):
        dst[...] = src[...].astype(BF16)

    dh = aw // ATT_HEADS
    nb = kmean_scr.shape[0]

    x = x_ref[0]
    xn = x * lax.rsqrt(jnp.mean(x * x, axis=-1, keepdims=True) + EPS)
    mod = mod_ref[0]
    h = (xn * g1_ref[...]) * (1.0 + mod[1:2]) + mod[0:1]
    hb = h.astype(BF16)

    def proj(lo, width):
        return _dot(hb, w_ref[:, lo:lo + width])

    def head_norm(t, gain):
        t2 = (t * t).astype(BF16)
        pw = pool_ref.shape[0]
        ms = jnp.concatenate([_dot(t2[:, c:c + pw], pool_ref[...]) for c in range(0, aw, pw)], axis=1)
        return t * lax.rsqrt(ms + EPS) * gain

    p_k = proj(4 * hw + aw, aw)
    p_q = proj(4 * hw, aw)
    p_v = proj(4 * hw + 2 * aw, aw)
    kn = head_norm(p_k, kg_ref[...])
    ak_ref[0] = kn.astype(BF16)
    per_tile = tm // BLOCK
    for s in range(per_tile):
        kmean_scr[pl.ds(i * per_tile + s, 1), :] = jnp.mean(
            kn[s * BLOCK:(s + 1) * BLOCK], axis=0, keepdims=True)

    p_hq = proj(0, hw)
    qn = head_norm(p_q, qg_ref[...])
    aq_ref[0] = (qn * (dh ** -0.5 * LOG2E)).astype(BF16)

    p_hf = proj(hw, hw)
    kmean = kmean_scr[...]
    rows = lax.broadcasted_iota(jnp.int32, (ATT_HEADS * nb, aw), 0)
    cols = lax.broadcasted_iota(jnp.int32, (ATT_HEADS * nb, aw), 1)
    sel = jnp.where(rows // nb == cols // dh, jnp.concatenate([kmean] * ATT_HEADS, axis=0), 0.0)
    sel_hi, sel_lo = _split2(sel)
    q_hi, q_lo = _split2(qn)
    both = _dot_nt(jnp.concatenate([sel_hi, sel_lo], axis=0), q_hi)
    n_sel = sel.shape[0]
    gate_ref[0] = (_dot_nt(sel_hi, q_lo) + both[n_sel:]) + both[:n_sel]
    avt_ref[0] = p_v.T.astype(BF16)

    p_hg = proj(3 * hw, hw)
    hq_ref[0] = _silu(p_hq).astype(BF16)
    lbl = lbl_ref[...]
    e = jnp.exp(lbl - jnp.max(lbl, axis=0, keepdims=True))
    lb = jnp.sum(e[0:layer + 1], axis=0, keepdims=True) / jnp.sum(e, axis=0, keepdims=True)
    p_hv = proj(2 * hw, hw)
    f = lb + (1.0 - lb) * jax.nn.sigmoid(p_hf)
    lf_ref[0] = jnp.log2(f)
    kk_ref[0] = (1.0 - f).astype(BF16)
    hg_ref[0] = _silu(p_hg).astype(BF16)
    hv_ref[0] = p_hv.astype(BF16)


def _cast_rows(rows, steps):
    chunk = BF16_ROWS
    while chunk * steps < rows or rows % chunk:
        chunk += BF16_ROWS
    return chunk


def _inproj(x, mod, g1, w_in, lb_logits, qg, kg, later_weights, *, layer, tm=512):
    bsz, t, d = x.shape
    hw = lb_logits.shape[1]
    aw = (w_in.shape[1] - 4 * hw) // 3
    dh = aw // ATT_HEADS
    nb = t // BLOCK
    tiles = t // tm
    lanes = jnp.arange(MXU_TILE)
    pool = jnp.where(lanes[:, None] // dh == lanes[None, :] // dh, 1.0 / dh, 0.0).astype(BF16)
    tok = lambda w: pl.BlockSpec((1, tm, w), lambda b, i: (b, i, 0))
    const = lambda shape: pl.BlockSpec(shape, lambda b, i: (0,) * len(shape), pipeline_mode=pl.Buffered(1))
    bf = lambda w: jax.ShapeDtypeStruct((bsz, t, w), BF16)
    tok_t = lambda rows: pl.BlockSpec((1, rows, tm), lambda b, i: (b, 0, i))

    def cast_spec(w):
        rows = _cast_rows(w.shape[0], bsz * tiles)
        last = w.shape[0] // rows - 1
        return pl.BlockSpec((rows, w.shape[1]), lambda b, i: (jnp.minimum(b * tiles + i, last), 0))

    cast_specs = [cast_spec(w) for w in later_weights]
    outs = pl.pallas_call(
        functools.partial(_inproj_kernel, tm=tm, hw=hw, aw=aw, layer=layer),
        out_shape=[bf(hw), bf(hw), jax.ShapeDtypeStruct((bsz, t, hw), F32), bf(hw), bf(hw),
                   bf(aw), bf(aw), jax.ShapeDtypeStruct((bsz, aw, t), BF16),
                   jax.ShapeDtypeStruct((bsz, ATT_HEADS * nb, t), F32)]
                  + [jax.ShapeDtypeStruct(w.shape, BF16) for w in later_weights],
        grid=(bsz, tiles),
        in_specs=[tok(d),
                  pl.BlockSpec((1, 6, d), lambda b, i: (b, 0, 0)),
                  const((1, d)), const(w_in.shape), const(lb_logits.shape),
                  const((1, aw)), const((1, aw)), const((MXU_TILE, MXU_TILE))] + cast_specs,
        out_specs=[tok(hw), tok(hw), tok(hw), tok(hw), tok(hw), tok(aw), tok(aw), tok_t(aw),
                   tok_t(ATT_HEADS * nb)] + cast_specs,
        scratch_shapes=[pltpu.VMEM((nb, aw), F32)],
        compiler_params=pltpu.CompilerParams(
            dimension_semantics=("arbitrary", "arbitrary"), vmem_limit_bytes=VMEM_LIMIT),
        name="inproj",
    )(x, mod, g1.reshape(1, d), w_in, lb_logits,
      jnp.tile(qg, ATT_HEADS).reshape(1, aw), jnp.tile(kg, ATT_HEADS).reshape(1, aw), pool, *later_weights)
    return outs[:9], outs[9:]


HG_LEVELS = (128, 64, 32, 16, 8)
HG_DIAG = 8
HG_EXP_BOUND = 100.0
HG_UNROLL = 8
BF16_ROWS = 16


def _group_rows(b, first, step, rows):
    n = b.shape[0] // rows
    return jnp.concatenate(
        [jnp.broadcast_to(b[first + g * step:first + g * step + 1], (rows, b.shape[1])) for g in range(n)], axis=0)


def _blend_rows(q, kk, m):
    n = q.shape[0] // m
    return jnp.concatenate([(q if g % 2 else kk)[g * m:(g + 1) * m] for g in range(n)], axis=0)


def _hgrn_tables(c_len):
    r = lax.broadcasted_iota(jnp.int32, (c_len, c_len), 0)
    c = lax.broadcasted_iota(jnp.int32, (c_len, c_len), 1)
    tri = (r >= c).astype(BF16)
    masks = [(r // (2 * m) == c // (2 * m)) & ((r // m) % 2 == 1) & ((c // m) % 2 == 0) for m in HG_LEVELS]
    masks.append((r // HG_DIAG == c // HG_DIAG) & (r >= c))
    return tri, masks


def _hgrn_factors(rows, q_ref, k_ref, lf_ref, v_ref, tri):
    c_len = HG_CHUNK
    q = q_ref[0, rows, :]
    kk = k_ref[0, rows, :]
    lf = lf_ref[0, rows, :]
    v = v_ref[0, rows, :]

    l1 = lf.astype(BF16)
    l2 = (lf - l1.astype(F32)).astype(BF16)
    b = _dot(tri, l2) + _dot(tri, l1)
    b_last = b[c_len - 1:c_len]

    q_in = q * jnp.exp2(b).astype(BF16)
    k_st = kk * jnp.exp2(b_last - b).astype(BF16)
    decay = jnp.exp2(b_last)
    pairs = []
    for m in HG_LEVELS:
        w = jnp.exp2(-jnp.abs(b - _group_rows(b, m - 1, 2 * m, 2 * m)))
        if m % BF16_ROWS == 0:
            x = _blend_rows(q, kk, m) * w.astype(BF16)
        else:
            x = (_blend_rows(q.astype(F32), kk.astype(F32), m) * w).astype(BF16)
        pairs.append((x, x))
    dd = b - _group_rows(b, HG_DIAG // 2, HG_DIAG, HG_DIAG)
    pairs.append((q * jnp.exp2(jnp.minimum(dd, HG_EXP_BOUND)).astype(BF16),
                  kk * jnp.exp2(jnp.minimum(-dd, HG_EXP_BOUND)).astype(BF16)))
    return q_in, k_st, decay, pairs, v


def _hgrn_attend(factors, hs, st, masks):
    q_in, k_st, decay, pairs, v = factors
    o_inter = _dot_nt(q_in[:, hs], st.astype(BF16))
    st_next = decay[:, hs] * st + _dot_tn(v[:, hs], k_st[:, hs])
    attn = jnp.zeros((HG_CHUNK, HG_CHUNK), F32)
    for (qw, kw), mask in zip(pairs, masks):
        attn = jnp.where(mask, _dot_nt(qw[:, hs], kw[:, hs]), attn)
    return o_inter, st_next, attn.astype(BF16)


def _hgrn_kernel(q_ref, k_ref, lf_ref, v_ref, g_ref, gain_ref, o_ref, st_ref):
    tri, masks = _hgrn_tables(HG_CHUNK)
    gain = gain_ref[...]
    d = st_ref.shape[1]
    st_ref[...] = jnp.zeros_like(st_ref)

    def body(ti, carry):
        base = ti * (HG_UNROLL * HG_CHUNK)
        rows = [pl.ds(pl.multiple_of(base + u * HG_CHUNK, HG_CHUNK), HG_CHUNK) for u in range(HG_UNROLL)]
        states = [st_ref[h] for h in range(HG_HEADS)]
        factors = {0: _hgrn_factors(rows[0], q_ref, k_ref, lf_ref, v_ref, tri)}
        pieces = {u: [] for u in range(HG_UNROLL)}

        def finish(u, hs, o_inter, attn):
            o = o_inter + _dot(attn, factors[u][4][:, hs])
            pieces[u].append(o * lax.rsqrt(jnp.mean(o * o, axis=-1, keepdims=True) + EPS) * gain)
            if len(pieces[u]) == HG_HEADS:
                out = jnp.concatenate(pieces[u], axis=1) * g_ref[0, rows[u], :].astype(F32)
                o_ref[0, rows[u], :] = out.astype(o_ref.dtype)

        pending = None
        for u in range(HG_UNROLL):
            if u + 1 < HG_UNROLL:
                factors[u + 1] = _hgrn_factors(rows[u + 1], q_ref, k_ref, lf_ref, v_ref, tri)
            for h in range(HG_HEADS):
                hs = slice(h * d, (h + 1) * d)
                o_inter, states[h], attn = _hgrn_attend(factors[u], hs, states[h], masks)
                if pending is not None:
                    finish(*pending)
                pending = (u, hs, o_inter, attn)
        finish(*pending)
        for h in range(HG_HEADS):
            st_ref[h] = states[h]
        return carry

    lax.fori_loop(0, q_ref.shape[1] // (HG_UNROLL * HG_CHUNK), body, 0)


def _hgrn(hq, kk, lf, hv, hg, gain):
    bsz, t, hw = hq.shape
    d = hw // HG_HEADS
    blk = pl.BlockSpec((1, t, hw), lambda b: (b, 0, 0))
    return pl.pallas_call(
        _hgrn_kernel,
        out_shape=jax.ShapeDtypeStruct((bsz, t, hw), BF16),
        grid=(bsz,),
        in_specs=[blk, blk, blk, blk, blk, pl.BlockSpec((1, d), lambda b: (0, 0))],
        out_specs=blk,
        scratch_shapes=[pltpu.VMEM((HG_HEADS, d, d), F32)],
        compiler_params=pltpu.CompilerParams(dimension_semantics=("parallel",), vmem_limit_bytes=VMEM_LIMIT),
        name="hgrn",
    )(hq, kk, lf, hv, hg, gain.reshape(1, d))


SLOPE_PARTS = 3
AUG_ROWS = 8
QUERY_TILE = 2 * MXU_TILE


def _moba_prepare(a, slope_ref, pair, qt, k_ref, vt_ref, gate_ref, kaug_scr, qaug_scr, vaug_scr):
    t_len, width = k_ref.shape[1], k_ref.shape[2]
    nb = t_len // BLOCK
    dh = width // 2
    k = k_ref[0]
    row = lax.broadcasted_iota(jnp.int32, k.shape, 0)
    ln = lax.broadcasted_iota(jnp.int32, k.shape, 1)
    in_block = (row % BLOCK).astype(F32)
    block_start = (row - row % BLOCK).astype(F32)
    feat = lax.broadcasted_iota(jnp.int32, (width, t_len), 0)
    jrow = lax.broadcasted_iota(jnp.int32, (nb, t_len), 0)
    qblk = lax.broadcasted_iota(jnp.int32, (nb, t_len), 1) // BLOCK
    crow = lax.broadcasted_iota(jnp.int32, (AUG_ROWS, t_len), 0)
    x0 = (1 - a) * dh
    y0 = x0 + AUG_ROWS
    extra = jnp.where(ln < x0 + SLOPE_PARTS, in_block,
                      jnp.where(ln < x0 + 2 * SLOPE_PARTS, block_start,
                                (ln - y0 == row // BLOCK).astype(F32)))
    kaug_scr[a] = jnp.where((ln >= x0) & (ln < y0 + nb), extra.astype(BF16), k)
    vaug_scr[a] = jnp.concatenate([vt_ref[0, a * dh:(a + 1) * dh, :], jnp.ones((BF16_ROWS, t_len), BF16)], axis=0)

    g = jnp.where(jrow < qblk, gate_ref[0, a * nb:(a + 1) * nb, :], -jnp.inf)
    rank = jnp.zeros((nb, t_len), jnp.int32)
    for i in range(nb - 1):
        gi = g[i:i + 1, :]
        rank = rank + ((gi > g) | ((gi == g) & (i < jrow))).astype(jnp.int32)
    sel_t = jnp.where((rank < TOPK) | (jrow >= qblk), 0.0, NEG)

    slope_t = jnp.zeros((AUG_ROWS, t_len), F32)
    for i in range(SLOPE_PARTS):
        slope_t = jnp.where((crow == i) | (crow == SLOPE_PARTS + i), slope_ref[i, 2 * pair + a], slope_t)
    pieces = [slope_t, sel_t, jnp.zeros((width - x0 - AUG_ROWS - nb, t_len), F32)]
    if x0:
        pieces.insert(0, jnp.zeros((x0, t_len), F32))
    own = (feat >= a * dh) & (feat < (a + 1) * dh)
    qaug_scr[a] = jnp.where(own, qt, jnp.concatenate(pieces, axis=0)).astype(BF16)


def _moba_scores(lo, a, kaug_scr, qaug_scr):
    hi = lo + QUERY_TILE
    key = lax.broadcasted_iota(jnp.int32, (QUERY_TILE, QUERY_TILE), 0)
    qry = lax.broadcasted_iota(jnp.int32, (QUERY_TILE, QUERY_TILE), 1)
    s = _dot(kaug_scr[a, 0:hi, :], qaug_scr[a, :, lo:hi])
    s_own = jnp.where(key <= qry, s[lo:], NEG)
    return s_own if lo == 0 else jnp.concatenate([s[:lo], s_own], axis=0)


def _moba_probs(s):
    m = jnp.max(s, axis=0, keepdims=True)
    return jnp.exp2(s - m).astype(BF16)


def _moba_values(lo, a, p, vaug_scr):
    dh = vaug_scr.shape[1] - BF16_ROWS
    acc = _dot(vaug_scr[a, :, 0:lo + QUERY_TILE], p)
    return acc[0:dh] / acc[dh:dh + 1]


def _moba_kernel(slope_ref, q_ref, k_ref, vt_ref, gate_ref, o_ref, kaug_scr, qaug_scr, vaug_scr):
    qt = q_ref[0].astype(F32).T
    for a in range(2):
        _moba_prepare(a, slope_ref, pl.program_id(1), qt, k_ref, vt_ref, gate_ref, kaug_scr, qaug_scr, vaug_scr)
    chains = [(lo, a) for lo in range(0, k_ref.shape[1], QUERY_TILE) for a in range(2)]
    scores, probs, outs = {}, {}, {}
    for i in range(len(chains) + 2):
        if i < len(chains):
            scores[i] = _moba_scores(*chains[i], kaug_scr, qaug_scr)
        if 1 <= i <= len(chains):
            probs[i - 1] = _moba_probs(scores.pop(i - 1))
        if i >= 2:
            lo, a = chains[i - 2]
            outs[a] = _moba_values(lo, a, probs.pop(i - 2), vaug_scr)
            if a == 1:
                o_ref[0, lo:lo + QUERY_TILE, :] = jnp.concatenate([outs[0], outs[1]], axis=0).T.astype(o_ref.dtype)


def _moba(aq, ak, av_t, gate_t):
    bsz, t, aw = aq.shape
    dh = aw // ATT_HEADS
    nb = t // BLOCK
    assert nb % 8 == 0 and AUG_ROWS + nb <= dh and 2 * SLOPE_PARTS <= AUG_ROWS and t % QUERY_TILE == 0
    slopes = LOG2E * jnp.exp2(-8.0 * jnp.arange(1, ATT_HEADS + 1, dtype=F32) / ATT_HEADS)
    parts = []
    for _ in range(SLOPE_PARTS):
        parts.append(slopes.astype(BF16).astype(F32))
        slopes = slopes - parts[-1]
    pair_spec = pl.BlockSpec((1, t, 2 * dh), lambda b, p: (b, 0, p))
    return pl.pallas_call(
        _moba_kernel,
        out_shape=jax.ShapeDtypeStruct((bsz, t, aw), BF16),
        grid=(bsz, ATT_HEADS // 2),
        in_specs=[pl.BlockSpec(memory_space=pltpu.SMEM),
                  pair_spec, pair_spec,
                  pl.BlockSpec((1, 2 * dh, t), lambda b, p: (b, p, 0)),
                  pl.BlockSpec((1, 2 * nb, t), lambda b, p: (b, p, 0))],
        out_specs=pair_spec,
        scratch_shapes=[pltpu.VMEM((2, t, 2 * dh), BF16), pltpu.VMEM((2, 2 * dh, t), BF16),
                        pltpu.VMEM((2, dh + BF16_ROWS, t), BF16)],
        compiler_params=pltpu.CompilerParams(
            dimension_semantics=("parallel", "parallel"), vmem_limit_bytes=VMEM_LIMIT),
        name="moba",
    )(jnp.stack(parts), aq, ak, av_t, gate_t)


def _ffn_kernel(x_ref, mh_ref, ma_ref, mod_ref, g2_ref, wo_ref, wg_ref, wu_ref, wd_ref, o_ref, *, ff_chunks):
    hw = mh_ref.shape[2]
    mod = mod_ref[0]
    tiles = wg_ref.shape[1] // MXU_TILE
    bounds = [-(-tiles * ci // ff_chunks) * MXU_TILE for ci in range(ff_chunks)] + [wg_ref.shape[1]]
    mixed = _dot(mh_ref[0], wo_ref[0:hw]) + _dot(ma_ref[0], wo_ref[hw:])
    x1 = x_ref[0] + mod[2:3] * mixed
    xn = x1 * lax.rsqrt(jnp.mean(x1 * x1, axis=-1, keepdims=True) + EPS)
    hb = ((xn * g2_ref[...]) * (1.0 + mod[4:5]) + mod[3:4]).astype(BF16)
    y = jnp.zeros_like(x1)
    for lo, hi in zip(bounds[:-1], bounds[1:]):
        act = _silu(_dot(hb, wg_ref[:, lo:hi])) * _dot(hb, wu_ref[:, lo:hi])
        y = y + _dot(act.astype(BF16), wd_ref[lo:hi])
    o_ref[0] = x1 + mod[5:6] * y


def _ffn(x, mh, ma, mod, g2, w_out, w_gate, w_up, w_down, *, tm=512, ff_chunks=2):
    bsz, t, d = x.shape
    tok = lambda w: pl.BlockSpec((1, tm, w), lambda b, i: (b, i, 0))
    const = lambda shape: pl.BlockSpec(shape, lambda b, i: (0,) * len(shape), pipeline_mode=pl.Buffered(1))
    return pl.pallas_call(
        functools.partial(_ffn_kernel, ff_chunks=ff_chunks),
        out_shape=jax.ShapeDtypeStruct((bsz, t, d), x.dtype),
        grid=(bsz, t // tm),
        in_specs=[tok(d), tok(mh.shape[2]), tok(ma.shape[2]),
                  pl.BlockSpec((1, 6, d), lambda b, i: (b, 0, 0)),
                  const((1, d)), const(w_out.shape), const(w_gate.shape), const(w_up.shape), const(w_down.shape)],
        out_specs=tok(d),
        compiler_params=pltpu.CompilerParams(
            dimension_semantics=("parallel", "parallel"), vmem_limit_bytes=VMEM_LIMIT),
        name="ffn",
    )(x, mh, ma, mod, g2.reshape(1, d), w_out, w_gate, w_up, w_down)


def kernel(x, c, w_ada, b_ada, norm1_g, w_in, lb_logits, hg_norm_g, q_norm_g, k_norm_g,
           w_out, norm2_g, w_gate, w_up, w_down):
    bsz, _, d = x.shape
    for l in range(w_ada.shape[0]):
        mod, w_in_bf = _adaln(c, w_ada[l], b_ada[l], w_in[l])
        mod = mod.reshape(bsz, 6, d)
        (hq, kk, lf, hv, hg, aq, ak, av_t, gate_t), ffn_weights = _inproj(
            x, mod, norm1_g[l], w_in_bf, lb_logits, q_norm_g[l], k_norm_g[l],
            (w_out[l], w_gate[l], w_up[l], w_down[l]), layer=l)
        o_hg = _hgrn(hq, kk, lf, hv, hg, hg_norm_g[l])
        o_att = _moba(aq, ak, av_t, gate_t)
        x = _ffn(x, o_hg, o_att, mod, norm2_g[l], *ffn_weights)
    return x
```

```python
import functools

import jax
import jax.numpy as jnp
from jax import lax
from jax.experimental import pallas as pl
from jax.experimental.pallas import tpu as pltpu

F32 = jnp.float32
BF16 = jnp.bfloat16

HG_HEADS = 4
ATT_HEADS = 8
BLOCK = 256
TOPK = 3
HG_CHUNK = 256
EPS = 1e-6
NEG = -(2.0 ** 100)

LOG2E = 1.4426950408889634
MXU_TILE = 256

VMEM_LIMIT = 56 * 1024 * 1024


def _silu(t):
    return t * jax.nn.sigmoid(t)


def _dot(a, b):
    return jnp.dot(a, b, preferred_element_type=F32)


def _dot_nt(a, b):
    return lax.dot_general(a, b, (((1,), (1,)), ((), ())), preferred_element_type=F32)


def _dot_tn(a, b):
    return lax.dot_general(a, b, (((0,), (0,)), ((), ())), preferred_element_type=F32)


def _split2(t):
    hi = t.astype(BF16)
    lo = (t - hi.astype(F32)).astype(BF16)
    return hi, lo


def _dot3(a, b, dot=_dot):
    a_hi, a_lo = _split2(a)
    b_hi, b_lo = _split2(b)
    rows = a.shape[0]
    both = dot(jnp.concatenate([a_hi, a_lo], axis=0), b_hi)
    return (dot(a_hi, b_lo) + both[rows:]) + both[:rows]


def _adaln_kernel(c_ref, w_ref, b_ref, win_ref, o_ref, win_bf_ref):
    o_ref[...] = _dot3(_silu(c_ref[...]), w_ref[...]) + b_ref[...]
    win_bf_ref[...] = win_ref[...].astype(BF16)


def _adaln(c, w, b, w_in):
    bsz, d = c.shape
    n = w.shape[1]
    tn = 512
    steps = n // tn
    assert w_in.shape[1] % tn == 0 and w_in.shape[1] // tn <= steps
    last = w_in.shape[1] // tn - 1
    cast_spec = pl.BlockSpec((w_in.shape[0], tn), lambda j: (0, jnp.minimum(j, last)))
    return pl.pallas_call(
        _adaln_kernel,
        out_shape=[jax.ShapeDtypeStruct((bsz, n), F32), jax.ShapeDtypeStruct(w_in.shape, BF16)],
        grid=(steps,),
        in_specs=[pl.BlockSpec((bsz, d), lambda j: (0, 0)),
                  pl.BlockSpec((d, tn), lambda j: (0, j)),
                  pl.BlockSpec((1, tn), lambda j: (0, j)),
                  cast_spec],
        out_specs=[pl.BlockSpec((bsz, tn), lambda j: (0, j)), cast_spec],
        compiler_params=pltpu.CompilerParams(dimension_semantics=("arbitrary",)),
        name="adaln",
    )(c, w, b.reshape(1, n), w_in)


def _inproj_kernel(x_ref, mod_ref, g1_ref, w_ref, lbl_ref, qg_ref, kg_ref, pool_ref,
                   f32_0, f32_1, f32_2, f32_3,
                   hq_ref, kk_ref, lf_ref, hv_ref, hg_ref, aq_ref, ak_ref, avt_ref, gate_ref,
                   bf16_0, bf16_1, bf16_2, bf16_3,
                   kmean_scr, *, tm, hw, aw, layer):
    i = pl.program_id(1)

    @pl.when(i == 0)
    def _():
        kmean_scr[...] = jnp.zeros_like(kmean_scr)

    for src, dst in ((f32_0, bf16_0), (f32_1, bf16_1), (f32_2, bf16_2), (f32_3, bf16_3)):
        dst[...] = src[...].astype(BF16)

    dh = aw // ATT_HEADS
    nb = kmean_scr.shape[0]

    x = x_ref[0]
    xn = x * lax.rsqrt(jnp.mean(x * x, axis=-1, keepdims=True) + EPS)
    mod = mod_ref[0]
    h = (xn * g1_ref[...]) * (1.0 + mod[1:2]) + mod[0:1]
    hb = h.astype(BF16)

    def proj(lo, width):
        return _dot(hb, w_ref[:, lo:lo + width])

    def head_norm(t, gain):
        t2 = (t * t).astype(BF16)
        pw = pool_ref.shape[0]
        ms = jnp.concatenate([_dot(t2[:, c:c + pw], pool_ref[...]) for c in range(0, aw, pw)], axis=1)
        return t * lax.rsqrt(ms + EPS) * gain

    p_k = proj(4 * hw + aw, aw)
    p_q = proj(4 * hw, aw)
    p_v = proj(4 * hw + 2 * aw, aw)
    kn = head_norm(p_k, kg_ref[...])
    ak_ref[0] = kn.astype(BF16)
    per_tile = tm // BLOCK
    for s in range(per_tile):
        kmean_scr[pl.ds(i * per_tile + s, 1), :] = jnp.mean(
            kn[s * BLOCK:(s + 1) * BLOCK], axis=0, keepdims=True)

    p_hq = proj(0, hw)
    qn = head_norm(p_q, qg_ref[...])
    aq_ref[0] = (qn * (dh ** -0.5 * LOG2E)).astype(BF16)

    p_hf = proj(hw, hw)
    kmean = kmean_scr[...]
    rows = lax.broadcasted_iota(jnp.int32, (ATT_HEADS * nb, aw), 0)
    cols = lax.broadcasted_iota(jnp.int32, (ATT_HEADS * nb, aw), 1)
    sel = jnp.where(rows // nb == cols // dh, jnp.concatenate([kmean] * ATT_HEADS, axis=0), 0.0)
    gate_ref[0] = _dot3(sel, qn, dot=_dot_nt)
    avt_ref[0] = p_v.T.astype(BF16)

    p_hg = proj(3 * hw, hw)
    hq_ref[0] = _silu(p_hq).astype(BF16)
    lbl = lbl_ref[...]
    e = jnp.exp(lbl - jnp.max(lbl, axis=0, keepdims=True))
    lb = jnp.sum(e[0:layer + 1], axis=0, keepdims=True) / jnp.sum(e, axis=0, keepdims=True)
    p_hv = proj(2 * hw, hw)
    f = lb + (1.0 - lb) * jax.nn.sigmoid(p_hf)
    lf_ref[0] = jnp.log2(f)
    kk_ref[0] = (1.0 - f).astype(BF16)
    hg_ref[0] = _silu(p_hg).astype(BF16)
    hv_ref[0] = p_hv.astype(BF16)


def _cast_rows(rows, steps):
    chunk = BF16_ROWS
    while chunk * steps < rows or rows % chunk:
        chunk += BF16_ROWS
    return chunk


def _inproj(x, mod, g1, w_in, lb_logits, qg, kg, later_weights, *, layer, tm=512):
    bsz, t, d = x.shape
    hw = lb_logits.shape[1]
    aw = (w_in.shape[1] - 4 * hw) // 3
    dh = aw // ATT_HEADS
    nb = t // BLOCK
    tiles = t // tm
    lanes = jnp.arange(MXU_TILE)
    pool = jnp.where(lanes[:, None] // dh == lanes[None, :] // dh, 1.0 / dh, 0.0).astype(BF16)
    tok = lambda w: pl.BlockSpec((1, tm, w), lambda b, i: (b, i, 0))
    const = lambda shape: pl.BlockSpec(shape, lambda b, i: (0,) * len(shape), pipeline_mode=pl.Buffered(1))
    bf = lambda w: jax.ShapeDtypeStruct((bsz, t, w), BF16)
    tok_t = lambda rows: pl.BlockSpec((1, rows, tm), lambda b, i: (b, 0, i))

    def cast_spec(w):
        rows = _cast_rows(w.shape[0], bsz * tiles)
        last = w.shape[0] // rows - 1
        return pl.BlockSpec((rows, w.shape[1]), lambda b, i: (jnp.minimum(b * tiles + i, last), 0))

    cast_specs = [cast_spec(w) for w in later_weights]
    outs = pl.pallas_call(
        functools.partial(_inproj_kernel, tm=tm, hw=hw, aw=aw, layer=layer),
        out_shape=[bf(hw), bf(hw), jax.ShapeDtypeStruct((bsz, t, hw), F32), bf(hw), bf(hw),
                   bf(aw), bf(aw), jax.ShapeDtypeStruct((bsz, aw, t), BF16),
                   jax.ShapeDtypeStruct((bsz, ATT_HEADS * nb, t), F32)]
                  + [jax.ShapeDtypeStruct(w.shape, BF16) for w in later_weights],
        grid=(bsz, tiles),
        in_specs=[tok(d),
                  pl.BlockSpec((1, 6, d), lambda b, i: (b, 0, 0)),
                  const((1, d)), const(w_in.shape), const(lb_logits.shape),
                  const((1, aw)), const((1, aw)), const((MXU_TILE, MXU_TILE))] + cast_specs,
        out_specs=[tok(hw), tok(hw), tok(hw), tok(hw), tok(hw), tok(aw), tok(aw), tok_t(aw),
                   tok_t(ATT_HEADS * nb)] + cast_specs,
        scratch_shapes=[pltpu.VMEM((nb, aw), F32)],
        compiler_params=pltpu.CompilerParams(
            dimension_semantics=("arbitrary", "arbitrary"), vmem_limit_bytes=VMEM_LIMIT),
        name="inproj",
    )(x, mod, g1.reshape(1, d), w_in, lb_logits,
      jnp.tile(qg, ATT_HEADS).reshape(1, aw), jnp.tile(kg, ATT_HEADS).reshape(1, aw), pool, *later_weights)
    return outs[:9], outs[9:]


HG_LEVELS = (128, 64, 32, 16, 8)
HG_DIAG = 8
HG_EXP_BOUND = 100.0
HG_UNROLL = 8
BF16_ROWS = 16


def _group_rows(b, first, step, rows):
    n = b.shape[0] // rows
    return jnp.concatenate(
        [jnp.broadcast_to(b[first + g * step:first + g * step + 1], (rows, b.shape[1])) for g in range(n)], axis=0)


def _blend_rows(q, kk, m):
    n = q.shape[0] // m
    return jnp.concatenate([(q if g % 2 else kk)[g * m:(g + 1) * m] for g in range(n)], axis=0)


def _hgrn_tables(c_len):
    r = lax.broadcasted_iota(jnp.int32, (c_len, c_len), 0)
    c = lax.broadcasted_iota(jnp.int32, (c_len, c_len), 1)
    tri = (r >= c).astype(BF16)
    masks = [(r // (2 * m) == c // (2 * m)) & ((r // m) % 2 == 1) & ((c // m) % 2 == 0) for m in HG_LEVELS]
    masks.append((r // HG_DIAG == c // HG_DIAG) & (r >= c))
    return tri, masks


def _hgrn_factors(rows, q_ref, k_ref, lf_ref, v_ref, tri):
    c_len = HG_CHUNK
    q = q_ref[0, rows, :]
    kk = k_ref[0, rows, :]
    lf = lf_ref[0, rows, :]
    v = v_ref[0, rows, :]

    l1 = lf.astype(BF16)
    l2 = (lf - l1.astype(F32)).astype(BF16)
    b = _dot(tri, l2) + _dot(tri, l1)
    b_last = b[c_len - 1:c_len]

    q_in = q * jnp.exp2(b).astype(BF16)
    k_st = kk * jnp.exp2(b_last - b).astype(BF16)
    decay = jnp.exp2(b_last)
    pairs = []
    for m in HG_LEVELS:
        w = jnp.exp2(-jnp.abs(b - _group_rows(b, m - 1, 2 * m, 2 * m)))
        if m % BF16_ROWS == 0:
            x = _blend_rows(q, kk, m) * w.astype(BF16)
        else:
            x = (_blend_rows(q.astype(F32), kk.astype(F32), m) * w).astype(BF16)
        pairs.append((x, x))
    dd = b - _group_rows(b, HG_DIAG // 2, HG_DIAG, HG_DIAG)
    pairs.append((q * jnp.exp2(jnp.minimum(dd, HG_EXP_BOUND)).astype(BF16),
                  kk * jnp.exp2(jnp.minimum(-dd, HG_EXP_BOUND)).astype(BF16)))
    return q_in, k_st, decay, pairs, v


def _hgrn_attend(factors, hs, st, masks):
    q_in, k_st, decay, pairs, v = factors
    o_inter = _dot_nt(q_in[:, hs], st.astype(BF16))
    st_next = decay[:, hs] * st + _dot_tn(v[:, hs], k_st[:, hs])
    attn = jnp.zeros((HG_CHUNK, HG_CHUNK), F32)
    for (qw, kw), mask in zip(pairs, masks):
        attn = jnp.where(mask, _dot_nt(qw[:, hs], kw[:, hs]), attn)
    return o_inter, st_next, attn.astype(BF16)


def _hgrn_kernel(q_ref, k_ref, lf_ref, v_ref, g_ref, gain_ref, o_ref, st_ref):
    tri, masks = _hgrn_tables(HG_CHUNK)
    gain = gain_ref[...]
    d = st_ref.shape[1]
    st_ref[...] = jnp.zeros_like(st_ref)

    def body(ti, carry):
        base = ti * (HG_UNROLL * HG_CHUNK)
        rows = [pl.ds(pl.multiple_of(base + u * HG_CHUNK, HG_CHUNK), HG_CHUNK) for u in range(HG_UNROLL)]
        states = [st_ref[h] for h in range(HG_HEADS)]
        factors = {0: _hgrn_factors(rows[0], q_ref, k_ref, lf_ref, v_ref, tri)}
        pieces = {u: [] for u in range(HG_UNROLL)}

        def finish(u, hs, o_inter, attn):
            o = o_inter + _dot(attn, factors[u][4][:, hs])
            pieces[u].append(o * lax.rsqrt(jnp.mean(o * o, axis=-1, keepdims=True) + EPS) * gain)
            if len(pieces[u]) == HG_HEADS:
                out = jnp.concatenate(pieces[u], axis=1) * g_ref[0, rows[u], :].astype(F32)
                o_ref[0, rows[u], :] = out.astype(o_ref.dtype)

        pending = None
        for u in range(HG_UNROLL):
            if u + 1 < HG_UNROLL:
                factors[u + 1] = _hgrn_factors(rows[u + 1], q_ref, k_ref, lf_ref, v_ref, tri)
            for h in range(HG_HEADS):
                hs = slice(h * d, (h + 1) * d)
                o_inter, states[h], attn = _hgrn_attend(factors[u], hs, states[h], masks)
                if pending is not None:
                    finish(*pending)
                pending = (u, hs, o_inter, attn)
        finish(*pending)
        for h in range(HG_HEADS):
            st_ref[h] = states[h]
        return carry

    lax.fori_loop(0, q_ref.shape[1] // (HG_UNROLL * HG_CHUNK), body, 0)


def _hgrn(hq, kk, lf, hv, hg, gain):
    bsz, t, hw = hq.shape
    d = hw // HG_HEADS
    blk = pl.BlockSpec((1, t, hw), lambda b: (b, 0, 0))
    return pl.pallas_call(
        _hgrn_kernel,
        out_shape=jax.ShapeDtypeStruct((bsz, t, hw), BF16),
        grid=(bsz,),
        in_specs=[blk, blk, blk, blk, blk, pl.BlockSpec((1, d), lambda b: (0, 0))],
        out_specs=blk,
        scratch_shapes=[pltpu.VMEM((HG_HEADS, d, d), F32)],
        compiler_params=pltpu.CompilerParams(dimension_semantics=("parallel",), vmem_limit_bytes=VMEM_LIMIT),
        name="hgrn",
    )(hq, kk, lf, hv, hg, gain.reshape(1, d))


SLOPE_PARTS = 3
AUG_ROWS = 8
QUERY_TILE = 2 * MXU_TILE


def _moba_prepare(a, slope_ref, pair, qt, k_ref, vt_ref, gate_ref, kaug_scr, qaug_scr, vaug_scr):
    t_len, width = k_ref.shape[1], k_ref.shape[2]
    nb = t_len // BLOCK
    dh = width // 2
    k = k_ref[0]
    row = lax.broadcasted_iota(jnp.int32, k.shape, 0)
    ln = lax.broadcasted_iota(jnp.int32, k.shape, 1)
    in_block = (row % BLOCK).astype(F32)
    block_start = (row - row % BLOCK).astype(F32)
    feat = lax.broadcasted_iota(jnp.int32, (width, t_len), 0)
    jrow = lax.broadcasted_iota(jnp.int32, (nb, t_len), 0)
    qblk = lax.broadcasted_iota(jnp.int32, (nb, t_len), 1) // BLOCK
    crow = lax.broadcasted_iota(jnp.int32, (AUG_ROWS, t_len), 0)
    x0 = (1 - a) * dh
    y0 = x0 + AUG_ROWS
    extra = jnp.where(ln < x0 + SLOPE_PARTS, in_block,
                      jnp.where(ln < x0 + 2 * SLOPE_PARTS, block_start,
                                (ln - y0 == row // BLOCK).astype(F32)))
    kaug_scr[a] = jnp.where((ln >= x0) & (ln < y0 + nb), extra.astype(BF16), k)
    vaug_scr[a] = jnp.concatenate([vt_ref[0, a * dh:(a + 1) * dh, :], jnp.ones((BF16_ROWS, t_len), BF16)], axis=0)

    g = jnp.where(jrow < qblk, gate_ref[0, a * nb:(a + 1) * nb, :], -jnp.inf)
    rank = jnp.zeros((nb, t_len), jnp.int32)
    for i in range(nb - 1):
        gi = g[i:i + 1, :]
        rank = rank + ((gi > g) | ((gi == g) & (i < jrow))).astype(jnp.int32)
    sel_t = jnp.where((rank < TOPK) | (jrow >= qblk), 0.0, NEG)

    slope_t = jnp.zeros((AUG_ROWS, t_len), F32)
    for i in range(SLOPE_PARTS):
        slope_t = jnp.where((crow == i) | (crow == SLOPE_PARTS + i), slope_ref[i, 2 * pair + a], slope_t)
    pieces = [slope_t, sel_t, jnp.zeros((width - x0 - AUG_ROWS - nb, t_len), F32)]
    if x0:
        pieces.insert(0, jnp.zeros((x0, t_len), F32))
    own = (feat >= a * dh) & (feat < (a + 1) * dh)
    qaug_scr[a] = jnp.where(own, qt, jnp.concatenate(pieces, axis=0)).astype(BF16)


def _moba_scores(lo, a, kaug_scr, qaug_scr):
    hi = lo + QUERY_TILE
    key = lax.broadcasted_iota(jnp.int32, (QUERY_TILE, QUERY_TILE), 0)
    qry = lax.broadcasted_iota(jnp.int32, (QUERY_TILE, QUERY_TILE), 1)
    s = _dot(kaug_scr[a, 0:hi, :], qaug_scr[a, :, lo:hi])
    s_own = jnp.where(key <= qry, s[lo:], NEG)
    return s_own if lo == 0 else jnp.concatenate([s[:lo], s_own], axis=0)


def _moba_probs(s):
    m = jnp.max(s, axis=0, keepdims=True)
    return jnp.exp2(s - m).astype(BF16)


def _moba_values(lo, a, p, vaug_scr):
    dh = vaug_scr.shape[1] - BF16_ROWS
    acc = _dot(vaug_scr[a, :, 0:lo + QUERY_TILE], p)
    return acc[0:dh] / acc[dh:dh + 1]


def _moba_kernel(slope_ref, q_ref, k_ref, vt_ref, gate_ref, o_ref, kaug_scr, qaug_scr, vaug_scr):
    qt = q_ref[0].astype(F32).T
    for a in range(2):
        _moba_prepare(a, slope_ref, pl.program_id(1), qt, k_ref, vt_ref, gate_ref, kaug_scr, qaug_scr, vaug_scr)
    chains = [(lo, a) for lo in range(0, k_ref.shape[1], QUERY_TILE) for a in range(2)]
    scores, probs, outs = {}, {}, {}
    for i in range(len(chains) + 2):
        if i < len(chains):
            scores[i] = _moba_scores(*chains[i], kaug_scr, qaug_scr)
        if 1 <= i <= len(chains):
            probs[i - 1] = _moba_probs(scores.pop(i - 1))
        if i >= 2:
            lo, a = chains[i - 2]
            outs[a] = _moba_values(lo, a, probs.pop(i - 2), vaug_scr)
            if a == 1:
                o_ref[0, lo:lo + QUERY_TILE, :] = jnp.concatenate([outs[0], outs[1]], axis=0).T.astype(o_ref.dtype)


def _moba(aq, ak, av_t, gate_t):
    bsz, t, aw = aq.shape
    dh = aw // ATT_HEADS
    nb = t // BLOCK
    assert nb % 8 == 0 and AUG_ROWS + nb <= dh and 2 * SLOPE_PARTS <= AUG_ROWS and t % QUERY_TILE == 0
    slopes = LOG2E * jnp.exp2(-8.0 * jnp.arange(1, ATT_HEADS + 1, dtype=F32) / ATT_HEADS)
    parts = []
    for _ in range(SLOPE_PARTS):
        parts.append(slopes.astype(BF16).astype(F32))
        slopes = slopes - parts[-1]
    pair_spec = pl.BlockSpec((1, t, 2 * dh), lambda b, p: (b, 0, p))
    return pl.pallas_call(
        _moba_kernel,
        out_shape=jax.ShapeDtypeStruct((bsz, t, aw), BF16),
        grid=(bsz, ATT_HEADS // 2),
        in_specs=[pl.BlockSpec(memory_space=pltpu.SMEM),
                  pair_spec, pair_spec,
                  pl.BlockSpec((1, 2 * dh, t), lambda b, p: (b, p, 0)),
                  pl.BlockSpec((1, 2 * nb, t), lambda b, p: (b, p, 0))],
        out_specs=pair_spec,
        scratch_shapes=[pltpu.VMEM((2, t, 2 * dh), BF16), pltpu.VMEM((2, 2 * dh, t), BF16),
                        pltpu.VMEM((2, dh + BF16_ROWS, t), BF16)],
        compiler_params=pltpu.CompilerParams(
            dimension_semantics=("parallel", "parallel"), vmem_limit_bytes=VMEM_LIMIT),
        name="moba",
    )(jnp.stack(parts), aq, ak, av_t, gate_t)


def _ffn_kernel(x_ref, mh_ref, ma_ref, mod_ref, g2_ref, wo_ref, wg_ref, wu_ref, wd_ref, o_ref, *, ff_chunks):
    hw = mh_ref.shape[2]
    mod = mod_ref[0]
    tiles = wg_ref.shape[1] // MXU_TILE
    bounds = [-(-tiles * ci // ff_chunks) * MXU_TILE for ci in range(ff_chunks)] + [wg_ref.shape[1]]
    mixed = _dot(mh_ref[0], wo_ref[0:hw]) + _dot(ma_ref[0], wo_ref[hw:])
    x1 = x_ref[0] + mod[2:3] * mixed
    xn = x1 * lax.rsqrt(jnp.mean(x1 * x1, axis=-1, keepdims=True) + EPS)
    hb = ((xn * g2_ref[...]) * (1.0 + mod[4:5]) + mod[3:4]).astype(BF16)
    y = jnp.zeros_like(x1)
    for lo, hi in zip(bounds[:-1], bounds[1:]):
        act = _silu(_dot(hb, wg_ref[:, lo:hi])) * _dot(hb, wu_ref[:, lo:hi])
        y = y + _dot(act.astype(BF16), wd_ref[lo:hi])
    o_ref[0] = x1 + mod[5:6] * y


def _ffn(x, mh, ma, mod, g2, w_out, w_gate, w_up, w_down, *, tm=512, ff_chunks=2):
    bsz, t, d = x.shape
    tok = lambda w: pl.BlockSpec((1, tm, w), lambda b, i: (b, i, 0))
    const = lambda shape: pl.BlockSpec(shape, lambda b, i: (0,) * len(shape), pipeline_mode=pl.Buffered(1))
    return pl.pallas_call(
        functools.partial(_ffn_kernel, ff_chunks=ff_chunks),
        out_shape=jax.ShapeDtypeStruct((bsz, t, d), x.dtype),
        grid=(bsz, t // tm),
        in_specs=[tok(d), tok(mh.shape[2]), tok(ma.shape[2]),
                  pl.BlockSpec((1, 6, d), lambda b, i: (b, 0, 0)),
                  const((1, d)), const(w_out.shape), const(w_gate.shape), const(w_up.shape), const(w_down.shape)],
        out_specs=tok(d),
        compiler_params=pltpu.CompilerParams(
            dimension_semantics=("parallel", "parallel"), vmem_limit_bytes=VMEM_LIMIT),
        name="ffn",
    )(x, mh, ma, mod, g2.reshape(1, d), w_out, w_gate, w_up, w_down)


def kernel(x, c, w_ada, b_ada, norm1_g, w_in, lb_logits, hg_norm_g, q_norm_g, k_norm_g,
           w_out, norm2_g, w_gate, w_up, w_down):
    bsz, _, d = x.shape
    for l in range(w_ada.shape[0]):
        mod, w_in_bf = _adaln(c, w_ada[l], b_ada[l], w_in[l])
        mod = mod.reshape(bsz, 6, d)
        (hq, kk, lf, hv, hg, aq, ak, av_t, gate_t), ffn_weights = _inproj(
            x, mod, norm1_g[l], w_in_bf, lb_logits, q_norm_g[l], k_norm_g[l],
            (w_out[l], w_gate[l], w_up[l], w_down[l]), layer=l)
        o_hg = _hgrn(hq, kk, lf, hv, hg, hg_norm_g[l])
        o_att = _moba(aq, ak, av_t, gate_t)
        x = _ffn(x, o_hg, o_att, mod, norm2_g[l], *ffn_weights)
    return x
```
